```python
import jax, jax.numpy as jnp
from jax import lax
import numpy as np

D_MODEL = 1024
BATCH = 8
SEQ = 2048
DEPTH = 1
DEC_BATCH = 2
DEC_SEQ = 8192
PAST_LEN = 128

GRID_W = 64
GLA_HEADS = 4
GLA_DK = 128
GLA_DV = 256
GLA_RANK = 16
GLA_GATE_NORM = 16.0
GLA_CHUNK = 64
GLA_QK = GLA_HEADS * GLA_DK
GLA_V = GLA_HEADS * GLA_DV
ATT_HEADS = 8
ATT_KV_HEADS = 2
ATT_HD = 128
ATT_Q = ATT_HEADS * ATT_HD
ATT_KV = ATT_KV_HEADS * ATT_HD
ROPE_THETA = 10000.0
Q_BLOCK = 128
N_EXPERTS = 16
CAPACITY_FACTOR = 2
EXPERT_FF = 2048
EPS = 1e-6
IN_SPLITS = (GLA_QK, GLA_QK, GLA_V, GLA_V, GLA_RANK, GLA_RANK, ATT_Q, ATT_KV, ATT_KV, D_MODEL, D_MODEL)
D_IN = GLA_QK * 2 + GLA_V * 2 + GLA_RANK * 2 + ATT_Q + ATT_KV * 2 + D_MODEL * 2

kernel_name = 'hybrid_gla_gqa_ec_encoder'


def _split_points():
    pts, acc = [], 0
    for w in IN_SPLITS[:-1]:
        acc += w
        pts.append(acc)
    return pts


def _rmsnorm(x, g):
    xf = x.astype(jnp.float32)
    y = xf * lax.rsqrt(jnp.mean(xf * xf, axis=-1, keepdims=True) + EPS) * g.astype(jnp.float32)
    return y.astype(x.dtype)


def _gla_one_direction(q, k, v, lg):
    B, H, S, dk = q.shape
    dv = v.shape[-1]
    C = GLA_CHUNK
    N = S // C
    q = q.reshape(B, H, N, C, dk)
    k = k.reshape(B, H, N, C, dk)
    v = v.reshape(B, H, N, C, dv)
    b = jnp.cumsum(lg.reshape(B, H, N, C, dk), axis=3)
    bT = b[:, :, :, -1:, :]
    q_t = q * jnp.exp(b)
    k_t = k * jnp.exp(-b)
    k_hat = k * jnp.exp(bT - b)
    mask = jnp.tril(jnp.ones((C, C), dtype=bool))
    A = jnp.where(mask, jnp.einsum('bhnid,bhnjd->bhnij', q_t, k_t), 0.0)
    o_intra = jnp.einsum('bhnij,bhnjv->bhniv', A, v)
    U = jnp.einsum('bhncd,bhncv->bhndv', k_hat, v)
    decay = jnp.exp(bT[:, :, :, 0, :])

    def step(state, inp):
        q_n, U_n, d_n = inp
        o_n = jnp.einsum('bhcd,bhdv->bhcv', q_n, state)
        return d_n[..., None] * state + U_n, o_n

    s0 = jnp.zeros((B, H, dk, dv), jnp.float32)
    _, o_inter = lax.scan(step, s0, (jnp.moveaxis(q_t, 2, 0), jnp.moveaxis(U, 2, 0), jnp.moveaxis(decay, 2, 0)))
    o = o_intra + jnp.moveaxis(o_inter, 0, 2)
    return o.reshape(B, H, S, dv)


def _gla_branch(q, k, v, g, r_f, r_b, w_gk_f, b_gk_f, w_gk_b, b_gk_b, gla_norm):
    B, S, _ = q.shape

    def heads(t, d):
        return t.reshape(B, S, GLA_HEADS, d).transpose(0, 2, 1, 3).astype(jnp.float32)

    qh = heads(q, GLA_DK) * (GLA_DK ** -0.5)
    kh = heads(k, GLA_DK)
    vh = heads(v, GLA_DV)
    lg_f = heads(jax.nn.log_sigmoid((r_f @ w_gk_f + b_gk_f).astype(jnp.float32)) / GLA_GATE_NORM, GLA_DK)
    lg_b = heads(jax.nn.log_sigmoid((r_b @ w_gk_b + b_gk_b).astype(jnp.float32)) / GLA_GATE_NORM, GLA_DK)
    o_f = _gla_one_direction(qh, kh, vh, lg_f)
    o_b = jnp.flip(_gla_one_direction(jnp.flip(qh, 2), jnp.flip(kh, 2), jnp.flip(vh, 2), jnp.flip(lg_b, 2)), 2)
    o = _rmsnorm(o_f + o_b, gla_norm)
    o = o.transpose(0, 2, 1, 3).reshape(B, S, GLA_V)
    return (o * jax.nn.silu(g.astype(jnp.float32))).astype(q.dtype)


def _axial_rope_tables(S):
    rows = S // GRID_W
    pos_r = jnp.repeat(jnp.arange(rows), GRID_W).astype(jnp.float32)
    pos_c = jnp.tile(jnp.arange(GRID_W), rows).astype(jnp.float32)
    half = ATT_HD // 2
    inv = ROPE_THETA ** (-jnp.arange(0, half, 2, dtype=jnp.float32) / half)
    ang = jnp.concatenate([pos_r[:, None] * inv, pos_c[:, None] * inv], axis=-1)
    return jnp.cos(ang), jnp.sin(ang)


def _apply_rope(x, cos, sin):
    xf = x.astype(jnp.float32).reshape(x.shape[:-1] + (ATT_HD // 2, 2))
    x0, x1 = xf[..., 0], xf[..., 1]
    out = jnp.stack([x0 * cos - x1 * sin, x0 * sin + x1 * cos], axis=-1)
    return out.reshape(x.shape)


def _gqa_branch(q, k, v, q_norm, k_norm):
    B, S, _ = q.shape
    G = ATT_HEADS // ATT_KV_HEADS
    qh = q.reshape(B, S, ATT_HEADS, ATT_HD).transpose(0, 2, 1, 3)
    kh = k.reshape(B, S, ATT_KV_HEADS, ATT_HD).transpose(0, 2, 1, 3)
    vh = v.reshape(B, S, ATT_KV_HEADS, ATT_HD).transpose(0, 2, 1, 3).astype(jnp.float32)
    cos, sin = _axial_rope_tables(S)
    qh = _apply_rope(_rmsnorm(qh, q_norm), cos, sin) * (ATT_HD ** -0.5)
    kh = _apply_rope(_rmsnorm(kh, k_norm), cos, sin)
    nb = S // Q_BLOCK
    qb = qh.reshape(B, ATT_KV_HEADS, G, nb, Q_BLOCK, ATT_HD).transpose(3, 0, 1, 2, 4, 5)

    def block(q_blk):
        s = jnp.einsum('bhgqd,bhkd->bhgqk', q_blk, kh)
        p = jax.nn.softmax(s, axis=-1)
        return jnp.einsum('bhgqk,bhkd->bhgqd', p, vh)

    o = lax.map(block, qb)
    o = o.transpose(1, 3, 4, 2, 0, 5)
    return o.reshape(B, S, ATT_Q).astype(q.dtype)


def _expert_choice_ffn(h, w_router, w_e_gate, w_e_up, w_e_down):
    B, S, D = h.shape
    T = B * S
    cap = CAPACITY_FACTOR * T // N_EXPERTS
    ht = h.reshape(T, D)
    aff = jax.nn.softmax((ht @ w_router).astype(jnp.float32), axis=-1)
    gate_vals, idx = lax.top_k(aff.T, cap)
    xe = ht[idx]
    hid = jax.nn.silu(jnp.einsum('ecd,edf->ecf', xe, w_e_gate)) * jnp.einsum('ecd,edf->ecf', xe, w_e_up)
    ye = jnp.einsum('ecf,efd->ecd', hid, w_e_down) * gate_vals[..., None].astype(h.dtype)
    out = jnp.zeros((T, D), ye.dtype).at[idx.reshape(-1)].add(ye.reshape(-1, D))
    return out.reshape(B, S, D).astype(h.dtype)


def _layer(x, norm_mix, w_in, w_gk_f, b_gk_f, w_gk_b, b_gk_b, gla_norm, q_norm, k_norm,
           w_o_gla, w_o_att, w_out, norm_ffn, w_router, w_e_gate, w_e_up, w_e_down):
    h = _rmsnorm(x, norm_mix)
    proj = h @ w_in
    gq, gk, gv, gg, r_f, r_b, aq, ak, av, z_a, z_b = jnp.split(proj, _split_points(), axis=-1)
    o_a = _gla_branch(gq, gk, gv, gg, r_f, r_b, w_gk_f, b_gk_f, w_gk_b, b_gk_b, gla_norm)
    o_b = _gqa_branch(aq, ak, av, q_norm, k_norm)
    merged = jax.nn.sigmoid(z_a) * (o_a @ w_o_gla) + jax.nn.sigmoid(z_b) * (o_b @ w_o_att)
    x = x + merged @ w_out
    x = x + _expert_choice_ffn(_rmsnorm(x, norm_ffn), w_router, w_e_gate, w_e_up, w_e_down)
    return x


def _trunk(x, params):
    for l in range(DEPTH):
        x = _layer(x, *[p[l] for p in params])
    return x


def setup_inputs(seed: int = 0) -> dict:
    key = jax.random.key(seed)
    ks = jax.random.split(key, 24)
    L = DEPTH

    def nrm(k, shape, scale):
        return jax.random.normal(k, shape, jnp.float32) * scale

    return {
        'x_prompt': nrm(ks[0], (BATCH, SEQ, D_MODEL), 1.0),
        'x_sample': nrm(ks[1], (DEC_BATCH, DEC_SEQ, D_MODEL), 1.0),
        'norm_mix': 1.0 + nrm(ks[2], (L, D_MODEL), 0.02),
        'w_in': nrm(ks[3], (L, D_MODEL, D_IN), D_MODEL ** -0.5),
        'w_gk_f': nrm(ks[4], (L, GLA_RANK, GLA_QK), GLA_RANK ** -0.5),
        'b_gk_f': nrm(ks[5], (L, GLA_QK), 0.1),
        'w_gk_b': nrm(ks[6], (L, GLA_RANK, GLA_QK), GLA_RANK ** -0.5),
        'b_gk_b': nrm(ks[7], (L, GLA_QK), 0.1),
        'gla_norm': 1.0 + nrm(ks[8], (L, GLA_DV), 0.02),
        'q_norm': 1.0 + nrm(ks[9], (L, ATT_HD), 0.02),
        'k_norm': 1.0 + nrm(ks[10], (L, ATT_HD), 0.02),
        'w_o_gla': nrm(ks[11], (L, GLA_V, D_MODEL), GLA_V ** -0.5),
        'w_o_att': nrm(ks[12], (L, ATT_Q, D_MODEL), ATT_Q ** -0.5),
        'w_out': nrm(ks[13], (L, D_MODEL, D_MODEL), D_MODEL ** -0.5),
        'norm_ffn': 1.0 + nrm(ks[14], (L, D_MODEL), 0.02),
        'w_router': nrm(ks[15], (L, D_MODEL, N_EXPERTS), D_MODEL ** -0.5),
        'w_e_gate': nrm(ks[16], (L, N_EXPERTS, D_MODEL, EXPERT_FF), D_MODEL ** -0.5),
        'w_e_up': nrm(ks[17], (L, N_EXPERTS, D_MODEL, EXPERT_FF), D_MODEL ** -0.5),
        'w_e_down': nrm(ks[18], (L, N_EXPERTS, EXPERT_FF, D_MODEL), EXPERT_FF ** -0.5),
    }


def reference(x_prompt, x_sample, norm_mix, w_in, w_gk_f, b_gk_f, w_gk_b, b_gk_b, gla_norm, q_norm, k_norm,
              w_o_gla, w_o_att, w_out, norm_ffn, w_router, w_e_gate, w_e_up, w_e_down):
    params = (norm_mix, w_in, w_gk_f, b_gk_f, w_gk_b, b_gk_b, gla_norm, q_norm, k_norm,
              w_o_gla, w_o_att, w_out, norm_ffn, w_router, w_e_gate, w_e_up, w_e_down)
    y_prompt = _trunk(x_prompt, params)
    y_sample = _trunk(x_sample, params)
    return (y_prompt, y_sample)
```

```python
import functools

import jax
import jax.numpy as jnp
import numpy as np
from jax import lax
from jax.experimental import pallas as pl
from jax.experimental.pallas import tpu as pltpu

D_MODEL = 1024
GRID_W = 64
GLA_HEADS = 4
GLA_DK = 128
GLA_DV = 256
GLA_RANK = 16
GLA_GATE_NORM = 16.0
GLA_CHUNK = 64
GLA_QK = GLA_HEADS * GLA_DK
GLA_V = GLA_HEADS * GLA_DV
ATT_HEADS = 8
ATT_KV_HEADS = 2
ATT_GROUP = ATT_HEADS // ATT_KV_HEADS
ATT_HD = 128
ATT_Q = ATT_HEADS * ATT_HD
ATT_KV = ATT_KV_HEADS * ATT_HD
ROPE_THETA = 10000.0
Q_BLOCK = 128
N_EXPERTS = 16
CAPACITY_FACTOR = 2
EXPERT_FF = 2048
EPS = 1e-6

LANE = 128
VMEM_LIMIT = 56 * 1024 * 1024

BF16 = jnp.bfloat16
F32 = jnp.float32

COL_GQ, COL_GK, COL_GV, COL_GG = 0, 4, 8, 16
COL_ZA, COL_ZB, COL_AQ, COL_AK, COL_AV = 24, 32, 40, 48, 50
N_MAIN_COLS = 52 * LANE


def _nt_dot(a, b, **kw):
    return lax.dot_general(a, b, (((1,), (1,)), ((), ())), preferred_element_type=F32, **kw)


def _tn_dot(a, b):
    return lax.dot_general(a, b, (((0,), (0,)), ((), ())), preferred_element_type=F32)


def _params(*sem):
    return pltpu.CompilerParams(dimension_semantics=sem, vmem_limit_bytes=VMEM_LIMIT)


def _inproj_kernel(x_ref, g_ref, wr_ref, w_ref, o_ref, r_ref, hn_ref):
    @pl.when(pl.program_id(1) == 0)
    def _():
        x = x_ref[...]
        ms = jnp.mean(x * x, axis=-1, keepdims=True)
        hn = (x * lax.rsqrt(ms + EPS) * g_ref[...]).astype(BF16)
        hn_ref[...] = hn
        r_ref[...] = jnp.dot(hn, wr_ref[...], preferred_element_type=F32)

    o_ref[...] = jnp.dot(hn_ref[...], w_ref[...], preferred_element_type=F32)


def _inproj(x, g, w_r, w_main, *, tm=1024, n_col_tiles=4):
    T = x.shape[0]
    tn = N_MAIN_COLS // n_col_tiles
    return pl.pallas_call(
        _inproj_kernel,
        grid=(T // tm, n_col_tiles),
        in_specs=[
            pl.BlockSpec((tm, D_MODEL), lambda i, j: (i, 0)),
            pl.BlockSpec((1, D_MODEL), lambda i, j: (0, 0)),
            pl.BlockSpec((D_MODEL, LANE), lambda i, j: (0, 0)),
            pl.BlockSpec((D_MODEL, tn), lambda i, j: (0, j)),
        ],
        out_specs=[
            pl.BlockSpec((tm, tn), lambda i, j: (i, j)),
            pl.BlockSpec((tm, LANE), lambda i, j: (i, 0)),
        ],
        out_shape=[
            jax.ShapeDtypeStruct((T, N_MAIN_COLS), F32),
            jax.ShapeDtypeStruct((T, LANE), F32),
        ],
        scratch_shapes=[pltpu.VMEM((tm, D_MODEL), BF16)],
        compiler_params=_params("parallel", "arbitrary"),
        name="inproj",
    )(x, g, w_r, w_main)


def _log_sigmoid(x):
    return jnp.minimum(x, 0.0) - jnp.log1p(jnp.exp(-jnp.abs(x)))


def _gla_kernel(q_ref, k_ref, v_ref, g_ref, r_ref, wf_ref, bf_ref, wb_ref, bb_ref, gn_ref,
                o_ref, of_ref, st_ref, *, L):
    C = GLA_CHUNK
    nc = L // C
    phase = pl.program_id(2)
    n = pl.program_id(3)
    nl = pl.num_programs(3)

    @pl.when(n == 0)
    def _():
        st_ref[...] = jnp.zeros_like(st_ref)

    row = lax.broadcasted_iota(jnp.int32, (C, C), 0)
    col = lax.broadcasted_iota(jnp.int32, (C, C), 1)

    def chunk(c, fwd):
        sl = pl.ds(pl.multiple_of(c * C, C), C)
        q = q_ref[0, sl, :]
        k = k_ref[0, sl, :]
        v = v_ref[0, sl, :].astype(BF16)
        r = r_ref[0, sl, :].astype(BF16)
        w_ref, b_ref = (wf_ref, bf_ref) if fwd else (wb_ref, bb_ref)
        pre = jnp.dot(r, w_ref[...], preferred_element_type=F32) + b_ref[...]
        lg = _log_sigmoid(pre) * (1.0 / GLA_GATE_NORM)
        mask = (col <= row) if fwd else (col >= row)
        b = jnp.dot(mask.astype(F32), lg, preferred_element_type=F32, precision=lax.Precision.HIGHEST)
        b_tot = b[C - 1:C, :] if fwd else b[0:1, :]
        q_t = (q * jnp.exp(b) * (GLA_DK ** -0.5)).astype(BF16)
        k_t = (k * jnp.exp(-b)).astype(BF16)
        k_hat = (k * jnp.exp(b_tot - b)).astype(BF16)
        a = jnp.where(mask, _nt_dot(q_t, k_t), 0.0).astype(BF16)
        st = st_ref[...]
        o = jnp.dot(a, v, preferred_element_type=F32) + _nt_dot(q_t, st.astype(BF16))
        st_ref[...] = st * jnp.exp(b_tot) + _tn_dot(v, k_hat)
        return sl, o

    @pl.when(phase == 0)
    def _():
        def body(c, carry):
            sl, o = chunk(c, True)
            base = pl.multiple_of(n * L + c * C, C)
            of_ref[pl.ds(base, C), :] = o
            return carry
        lax.fori_loop(0, nc, body, 0)

    @pl.when(phase == 1)
    def _():
        blk = nl - 1 - n

        def body(i, carry):
            c = nc - 1 - i
            sl, o_b = chunk(c, False)
            base = pl.multiple_of(blk * L + c * C, C)
            o = of_ref[pl.ds(base, C), :] + o_b
            ms = jnp.mean(o * o, axis=-1, keepdims=True)
            o = o * lax.rsqrt(ms + EPS) * gn_ref[...]
            g = g_ref[0, sl, :]
            o_ref[0, sl, :] = (o * (g * jax.nn.sigmoid(g))).astype(o_ref.dtype)
            return carry
        lax.fori_loop(0, nc, body, 0)


def _gla(proj3, r3, wf, bf, wb, bb, gn, *, L=512):
    B, S, _ = proj3.shape
    nl = S // L

    def blk(phase, n):
        return n + phase * (nl - 1 - 2 * n)

    def in_map(col0, width):
        return lambda b, h, p, n: (b, blk(p, n), col0 * LANE // width + h)

    return pl.pallas_call(
        functools.partial(_gla_kernel, L=L),
        grid=(B, GLA_HEADS, 2, nl),
        in_specs=[
            pl.BlockSpec((1, L, GLA_DK), in_map(COL_GQ, GLA_DK)),
            pl.BlockSpec((1, L, GLA_DK), in_map(COL_GK, GLA_DK)),
            pl.BlockSpec((1, L, GLA_DV), in_map(COL_GV, GLA_DV)),
            pl.BlockSpec((1, L, GLA_DV), in_map(COL_GG, GLA_DV)),
            pl.BlockSpec((1, L, LANE), lambda b, h, p, n: (b, blk(p, n), 0)),
            pl.BlockSpec((LANE, GLA_DK), lambda b, h, p, n: (0, h)),
            pl.BlockSpec((1, GLA_DK), lambda b, h, p, n: (0, h)),
            pl.BlockSpec((LANE, GLA_DK), lambda b, h, p, n: (0, h)),
            pl.BlockSpec((1, GLA_DK), lambda b, h, p, n: (0, h)),
            pl.BlockSpec((1, GLA_DV), lambda b, h, p, n: (0, 0)),
        ],
        out_specs=pl.BlockSpec((1, L, GLA_DV), lambda b, h, p, n: (b, nl - 1 - p * n, h)),
        out_shape=jax.ShapeDtypeStruct((B, S, GLA_V), BF16),
        scratch_shapes=[pltpu.VMEM((S, GLA_DV), F32), pltpu.VMEM((GLA_DV, GLA_DK), F32)],
        compiler_params=_params("parallel", "parallel", "arbitrary", "arbitrary"),
        name="gla",
    )(proj3, proj3, proj3, proj3, r3, wf, bf, wb, bb, gn)


def _rope_tables(S):
    rows = S // GRID_W
    pos_r = np.repeat(np.arange(rows), GRID_W).astype(np.float32)
    pos_c = np.tile(np.arange(GRID_W), rows).astype(np.float32)
    half = ATT_HD // 2
    inv = jnp.asarray(ROPE_THETA, F32) ** (-jnp.arange(0, half, 2, dtype=F32) / half)
    ang = jnp.concatenate([pos_r[:, None] * inv, pos_c[:, None] * inv], axis=-1)
    cos, sin = jnp.cos(ang), jnp.sin(ang)
    return jnp.concatenate([cos, cos], axis=-1), jnp.concatenate([-sin, sin], axis=-1)


def _qkprep_kernel(aq_ref, akv_ref, cos_ref, sin_ref, qn_ref, kn_ref, q_out, k_out, v_out):
    cos = cos_ref[...]
    sin = sin_ref[...]

    def norm_rope(x, w, scale):
        ms = jnp.mean(x * x, axis=-1, keepdims=True)
        y = x * lax.rsqrt(ms + EPS) * w
        return (y * cos + pltpu.roll(y, ATT_HD // 2, 1) * sin) * scale

    for h in range(ATT_HEADS):
        sl = slice(h * ATT_HD, (h + 1) * ATT_HD)
        q_out[:, sl] = norm_rope(aq_ref[:, sl], qn_ref[...], ATT_HD ** -0.5).astype(BF16)
    for h in range(ATT_KV_HEADS):
        sl = slice(h * ATT_HD, (h + 1) * ATT_HD)
        k_out[:, sl] = norm_rope(akv_ref[:, sl], kn_ref[...], 1.0).astype(BF16)
    v_out[...] = akv_ref[:, ATT_KV:2 * ATT_KV].astype(BF16)


def _qkprep(proj, cos, sin, qn, kn, S, *, tm=512):
    T = proj.shape[0]
    ns = S // tm
    return pl.pallas_call(
        _qkprep_kernel,
        grid=(T // tm,),
        in_specs=[
            pl.BlockSpec((tm, ATT_Q), lambda i: (i, COL_AQ * LANE // ATT_Q)),
            pl.BlockSpec((tm, 2 * ATT_KV), lambda i: (i, COL_AK * LANE // (2 * ATT_KV))),
            pl.BlockSpec((tm, ATT_HD), lambda i: (i % ns, 0)),
            pl.BlockSpec((tm, ATT_HD), lambda i: (i % ns, 0)),
            pl.BlockSpec((1, ATT_HD), lambda i: (0, 0)),
            pl.BlockSpec((1, ATT_HD), lambda i: (0, 0)),
        ],
        out_specs=[
            pl.BlockSpec((tm, ATT_Q), lambda i: (i, 0)),
            pl.BlockSpec((tm, ATT_KV), lambda i: (i, 0)),
            pl.BlockSpec((tm, ATT_KV), lambda i: (i, 0)),
        ],
        out_shape=[
            jax.ShapeDtypeStruct((T, ATT_Q), BF16),
            jax.ShapeDtypeStruct((T, ATT_KV), BF16),
            jax.ShapeDtypeStruct((T, ATT_KV), BF16),
        ],
        compiler_params=_params("parallel"),
        name="qkprep",
    )(proj, proj, cos, sin, qn, kn)


def _flash_kernel(q_ref, k_ref, v_ref, o_ref, qs_ref, m_ref, l_ref, acc_ref, *, tq):
    ki = pl.program_id(3)

    @pl.when(ki == 0)
    def _():
        for g in range(ATT_GROUP):
            qs_ref[g * tq:(g + 1) * tq, :] = q_ref[0, :, g * ATT_HD:(g + 1) * ATT_HD]
        m_ref[...] = jnp.full_like(m_ref, -jnp.inf)
        l_ref[...] = jnp.zeros_like(l_ref)
        acc_ref[...] = jnp.zeros_like(acc_ref)

    s = _nt_dot(qs_ref[...], k_ref[0])
    m_prev = m_ref[...]
    m_new = jnp.maximum(m_prev, jnp.max(s, axis=-1, keepdims=True))
    alpha = jnp.exp(m_prev - m_new)
    p = jnp.exp(s - m_new)
    l_ref[...] = alpha * l_ref[...] + jnp.sum(p, axis=-1, keepdims=True)
    acc_ref[...] = alpha * acc_ref[...] + jnp.dot(p.astype(BF16), v_ref[0], preferred_element_type=F32)
    m_ref[...] = m_new

    @pl.when(ki == pl.num_programs(3) - 1)
    def _():
        out = acc_ref[...] / l_ref[...]
        for g in range(ATT_GROUP):
            for u in range(tq // Q_BLOCK):
                rows = slice(g * tq + u * Q_BLOCK, g * tq + (u + 1) * Q_BLOCK)
                o_ref[0, g, :, u * ATT_HD:(u + 1) * ATT_HD] = out[rows, :].astype(o_ref.dtype)


def _flash(q3, k3, v3, *, tq=256, tk=1024):
    B, S, _ = q3.shape
    gw = ATT_GROUP * ATT_HD
    n_qt = S // tq
    return pl.pallas_call(
        functools.partial(_flash_kernel, tq=tq),
        grid=(B, ATT_KV_HEADS, S // tq, S // tk),
        in_specs=[
            pl.BlockSpec((1, tq, gw), lambda b, h, qi, ki: (b, qi, h)),
            pl.BlockSpec((1, tk, ATT_HD), lambda b, h, qi, ki: (b, ki, h)),
            pl.BlockSpec((1, tk, ATT_HD), lambda b, h, qi, ki: (b, ki, h)),
        ],
        out_specs=pl.BlockSpec((1, ATT_GROUP, Q_BLOCK, tq), lambda b, h, qi, ki: (b, 0, 0, h * n_qt + qi)),
        out_shape=jax.ShapeDtypeStruct((B, ATT_GROUP, Q_BLOCK, ATT_KV_HEADS * S), BF16),
        scratch_shapes=[
            pltpu.VMEM((ATT_GROUP * tq, ATT_HD), BF16),
            pltpu.VMEM((ATT_GROUP * tq, 1), F32),
            pltpu.VMEM((ATT_GROUP * tq, 1), F32),
            pltpu.VMEM((ATT_GROUP * tq, ATT_HD), F32),
        ],
        compiler_params=_params("parallel", "parallel", "parallel", "arbitrary"),
        name="flash",
    )(q3, k3, v3)


def _merge_kernel(x_ref, oa_ref, ob_ref, za_ref, zb_ref, woa_ref, wob_ref, wout_ref, nf_ref, wrt_ref,
                  x1_ref, h2_ref, aff_ref):
    ya = jnp.dot(oa_ref[...], woa_ref[...], preferred_element_type=F32)
    yb = jnp.dot(ob_ref[...], wob_ref[...], preferred_element_type=F32)
    merged = jax.nn.sigmoid(za_ref[...]) * ya + jax.nn.sigmoid(zb_ref[...]) * yb
    x1 = x_ref[...] + jnp.dot(merged.astype(BF16), wout_ref[...], preferred_element_type=F32)
    x1_ref[...] = x1
    ms = jnp.mean(x1 * x1, axis=-1, keepdims=True)
    h2 = x1 * lax.rsqrt(ms + EPS) * nf_ref[...]
    h2_ref[...] = h2.astype(BF16)
    logits = _nt_dot(wrt_ref[...], h2, precision=lax.Precision.HIGHEST)
    e = jnp.exp(logits - jnp.max(logits, axis=0, keepdims=True))
    aff_ref[...] = e / jnp.sum(e, axis=0, keepdims=True)


def _merge(x, o_a, o_b, proj, w_oa, w_ob, w_out, nf, w_rt, *, tm=512):
    T = x.shape[0]
    full = lambda i: (0, 0)
    return pl.pallas_call(
        _merge_kernel,
        grid=(T // tm,),
        in_specs=[
            pl.BlockSpec((tm, D_MODEL), lambda i: (i, 0)),
            pl.BlockSpec((tm, GLA_V), lambda i: (i, 0)),
            pl.BlockSpec((tm, ATT_Q), lambda i: (i, 0)),
            pl.BlockSpec((tm, D_MODEL), lambda i: (i, COL_ZA * LANE // D_MODEL)),
            pl.BlockSpec((tm, D_MODEL), lambda i: (i, COL_ZB * LANE // D_MODEL)),
            pl.BlockSpec((GLA_V, D_MODEL), full),
            pl.BlockSpec((ATT_Q, D_MODEL), full),
            pl.BlockSpec((D_MODEL, D_MODEL), full),
            pl.BlockSpec((1, D_MODEL), full),
            pl.BlockSpec((N_EXPERTS, D_MODEL), full),
        ],
        out_specs=[
            pl.BlockSpec((tm, D_MODEL), lambda i: (i, 0)),
            pl.BlockSpec((tm, D_MODEL), lambda i: (i, 0)),
            pl.BlockSpec((N_EXPERTS, tm), lambda i: (0, i)),
        ],
        out_shape=[
            jax.ShapeDtypeStruct((T, D_MODEL), F32),
            jax.ShapeDtypeStruct((T, D_MODEL), BF16),
            jax.ShapeDtypeStruct((N_EXPERTS, T), F32),
        ],
        compiler_params=_params("parallel"),
        name="merge",
    )(x, o_a, o_b, proj, proj, w_oa, w_ob, w_out, nf, w_rt)


def _ffn_kernel(xe_ref, gate_ref, wg_ref, wu_ref, wd_ref, o_ref, *, rows):
    f = pl.program_id(1)
    wg = wg_ref[0].astype(BF16)
    wu = wu_ref[0].astype(BF16)
    wd = wd_ref[0].astype(BF16)
    cap = xe_ref.shape[1]

    def body(i, carry):
        sl = pl.ds(pl.multiple_of(i * rows, rows), rows)
        xe = xe_ref[0, sl, :]
        hg = jnp.dot(xe, wg, preferred_element_type=F32)
        hu = jnp.dot(xe, wu, preferred_element_type=F32)
        hid = (hg * jax.nn.sigmoid(hg) * hu).astype(BF16)
        y = jnp.dot(hid, wd, preferred_element_type=F32)

        @pl.when(f == 0)
        def _():
            o_ref[0, sl, :] = y

        @pl.when(f != 0)
        def _():
            o_ref[0, sl, :] += y
        return carry

    lax.fori_loop(0, cap // rows, body, 0)

    @pl.when(f == pl.num_programs(1) - 1)
    def _():
        o_ref[0] = o_ref[0] * gate_ref[0]


def _ffn(xe, gate, w_g, w_u, w_d, *, tf=512, rows=256):
    E, cap, _ = xe.shape
    return pl.pallas_call(
        functools.partial(_ffn_kernel, rows=rows),
        grid=(E, EXPERT_FF // tf),
        in_specs=[
            pl.BlockSpec((1, cap, D_MODEL), lambda e, f: (e, 0, 0)),
            pl.BlockSpec((1, cap, 1), lambda e, f: (e, 0, 0)),
            pl.BlockSpec((1, D_MODEL, tf), lambda e, f: (e, 0, f)),
            pl.BlockSpec((1, D_MODEL, tf), lambda e, f: (e, 0, f)),
            pl.BlockSpec((1, tf, D_MODEL), lambda e, f: (e, f, 0)),
        ],
        out_specs=pl.BlockSpec((1, cap, D_MODEL), lambda e, f: (e, 0, 0)),
        out_shape=jax.ShapeDtypeStruct((E, cap, D_MODEL), F32),
        compiler_params=_params("parallel", "arbitrary"),
        name="ffn",
    )(xe, gate, w_g, w_u, w_d)


def _prep_weights(w_in, w_gk_f, w_gk_b, q_norm, k_norm):
    pts = np.cumsum([GLA_QK, GLA_QK, GLA_V, GLA_V, GLA_RANK, GLA_RANK, ATT_Q, ATT_KV, ATT_KV, D_MODEL])
    gq, gk, gv, gg, r_f, r_b, aq, ak, av, z_a, z_b = jnp.split(w_in, pts, axis=-1)
    perm = np.concatenate([np.arange(0, ATT_HD, 2), np.arange(1, ATT_HD, 2)])
    perm_q = (np.arange(ATT_HEADS)[:, None] * ATT_HD + perm[None, :]).reshape(-1)
    perm_k = (np.arange(ATT_KV_HEADS)[:, None] * ATT_HD + perm[None, :]).reshape(-1)
    w_main = jnp.concatenate([gq, gk, gv, gg, z_a, z_b, aq[:, perm_q], ak[:, perm_k], av], axis=-1)
    w_r = jnp.pad(jnp.concatenate([r_f, r_b], axis=-1), ((0, 0), (0, LANE - 2 * GLA_RANK)))
    wf = jnp.pad(w_gk_f, ((0, LANE - GLA_RANK), (0, 0)))
    wb = jnp.pad(w_gk_b, ((GLA_RANK, LANE - 2 * GLA_RANK), (0, 0)))
    return (w_main.astype(BF16), w_r.astype(BF16), wf.astype(BF16), wb.astype(BF16),
            q_norm[perm][None, :], k_norm[perm][None, :])


def _layer(x3, norm_mix, w_main, w_r, wf, bf, wb, bb, gla_norm, qn, kn,
           w_oa, w_ob, w_out, norm_ffn, w_rt, w_e_gate, w_e_up, w_e_down):
    B, S, D = x3.shape
    T = B * S
    x = x3.reshape(T, D)
    proj, r = _inproj(x, norm_mix, w_r, w_main)
    o_a = _gla(proj.reshape(B, S, -1), r.reshape(B, S, LANE), wf, bf, wb, bb, gla_norm)
    cos, sin = _rope_tables(S)
    q, k, v = _qkprep(proj, cos, sin, qn, kn, S)
    o_b = _flash(q.reshape(B, S, ATT_Q), k.reshape(B, S, ATT_KV), v.reshape(B, S, ATT_KV))
    x1, h2, aff_t = _merge(x, o_a.reshape(T, GLA_V), o_b.reshape(T, ATT_Q), proj,
                           w_oa, w_ob, w_out, norm_ffn, w_rt)
    cap = CAPACITY_FACTOR * T // N_EXPERTS
    gate_vals, idx = lax.top_k(aff_t, cap)
    xe = h2[idx]
    ye = _ffn(xe, gate_vals[..., None], w_e_gate, w_e_up, w_e_down)
    out = x1.at[idx.reshape(-1)].add(ye.reshape(-1, D))
    return out.reshape(B, S, D)


def kernel(x_prompt, x_sample, norm_mix, w_in, w_gk_f, b_gk_f, w_gk_b, b_gk_b, gla_norm, q_norm, k_norm,
           w_o_gla, w_o_att, w_out, norm_ffn, w_router, w_e_gate, w_e_up, w_e_down):
    w_main, w_r, wf, wb, qn, kn = _prep_weights(w_in[0], w_gk_f[0], w_gk_b[0], q_norm[0], k_norm[0])
    args = (norm_mix[0][None, :], w_main, w_r, wf, b_gk_f[0][None, :], wb, b_gk_b[0][None, :],
            gla_norm[0][None, :], qn, kn,
            w_o_gla[0].astype(BF16), w_o_att[0].astype(BF16), w_out[0].astype(BF16),
            norm_ffn[0][None, :], w_router[0].T, w_e_gate[0], w_e_up[0], w_e_down[0])
    return (_layer(x_prompt, *args), _layer(x_sample, *args))
```

```python
import functools

import jax
import jax.numpy as jnp
import numpy as np
from jax import lax
from jax.experimental import pallas as pl
from jax.experimental.pallas import tpu as pltpu

D_MODEL = 1024
GRID_W = 64
GLA_HEADS = 4
GLA_DK = 128
GLA_DV = 256
GLA_RANK = 16
GLA_GATE_NORM = 16.0
GLA_CHUNK = 64
GLA_QK = GLA_HEADS * GLA_DK
GLA_V = GLA_HEADS * GLA_DV
ATT_HEADS = 8
ATT_KV_HEADS = 2
ATT_GROUP = ATT_HEADS // ATT_KV_HEADS
ATT_HD = 128
ATT_Q = ATT_HEADS * ATT_HD
ATT_KV = ATT_KV_HEADS * ATT_HD
ROPE_THETA = 10000.0
Q_BLOCK = 128
N_EXPERTS = 16
CAPACITY_FACTOR = 2
EXPERT_FF = 2048
EPS = 1e-6

LANE = 128
VMEM_LIMIT = 56 * 1024 * 1024

BF16 = jnp.bfloat16
F32 = jnp.float32
I32 = jnp.int32

COL_GQ, COL_GK, COL_GV, COL_GG = 0, 4, 8, 16
COL_ZA, COL_ZB, COL_AQ, COL_AK, COL_AV = 24, 32, 40, 48, 50
N_MAIN_COLS = 52 * LANE

GLA_BLOCK = 256
GLA_SUB = GLA_BLOCK // GLA_CHUNK
ROUTE_TB = 256
ROUTE_WIN = 128
ROUTE_NWIN = ROUTE_TB // ROUTE_WIN + 1


def _nt_dot(a, b, **kw):
    return lax.dot_general(a, b, (((1,), (1,)), ((), ())), preferred_element_type=F32, **kw)


def _tn_dot(a, b):
    return lax.dot_general(a, b, (((0,), (0,)), ((), ())), preferred_element_type=F32)


def _dot(a, b):
    return jnp.dot(a, b, preferred_element_type=F32)


def _params(*sem):
    return pltpu.CompilerParams(dimension_semantics=sem, vmem_limit_bytes=VMEM_LIMIT)


def _inproj_kernel(x_ref, g_ref, wr_ref, w_ref, o_ref, r_ref, hn_ref):
    @pl.when(pl.program_id(1) == 0)
    def _():
        x = x_ref[...]
        ms = jnp.mean(x * x, axis=-1, keepdims=True)
        hn = (x * lax.rsqrt(ms + EPS) * g_ref[...]).astype(BF16)
        hn_ref[...] = hn
        r_ref[...] = _dot(hn, wr_ref[...])

    o_ref[...] = _dot(hn_ref[...], w_ref[...])


def _inproj(x, g, w_r, w_main, *, tm=1024, n_col_tiles=4):
    T = x.shape[0]
    tn = N_MAIN_COLS // n_col_tiles
    return pl.pallas_call(
        _inproj_kernel,
        grid=(T // tm, n_col_tiles),
        in_specs=[
            pl.BlockSpec((tm, D_MODEL), lambda i, j: (i, 0)),
            pl.BlockSpec((1, D_MODEL), lambda i, j: (0, 0)),
            pl.BlockSpec((D_MODEL, LANE), lambda i, j: (0, 0)),
            pl.BlockSpec((D_MODEL, tn), lambda i, j: (0, j)),
        ],
        out_specs=[
            pl.BlockSpec((tm, tn), lambda i, j: (i, j)),
            pl.BlockSpec((tm, LANE), lambda i, j: (i, 0)),
        ],
        out_shape=[
            jax.ShapeDtypeStruct((T, N_MAIN_COLS), F32),
            jax.ShapeDtypeStruct((T, LANE), F32),
        ],
        scratch_shapes=[pltpu.VMEM((tm, D_MODEL), BF16)],
        compiler_params=_params("parallel", "arbitrary"),
        name="inproj",
    )(x, g, w_r, w_main)


def _log_sigmoid(x):
    return jnp.minimum(x, 0.0) - jnp.log1p(jnp.exp(-jnp.abs(x)))


def _rows(parts):
    return jnp.concatenate([jnp.broadcast_to(p, (GLA_CHUNK, p.shape[-1])) for p in parts], axis=0)


def _gla_masks(fwd):
    n = GLA_BLOCK
    row = lax.broadcasted_iota(I32, (n, n), 0)
    col = lax.broadcasted_iota(I32, (n, n), 1)
    same = (row // GLA_CHUNK) == (col // GLA_CHUNK)
    tri = same & ((col <= row) if fwd else (col >= row))
    dist = (row // GLA_CHUNK - col // GLA_CHUNK) * (1 if fwd else -1)
    return tri, dist


def _gla_block(q, k, v, r, w, bias, st, tri, dist, fwd):
    C = GLA_CHUNK
    lg = _log_sigmoid(_dot(r, w) + bias) * (1.0 / GLA_GATE_NORM)
    hi = lg.astype(BF16)
    rem = lg - hi.astype(F32)
    mid = rem.astype(BF16)
    lo = (rem - mid.astype(F32)).astype(BF16)
    tri_b = jnp.where(tri, 1.0, 0.0).astype(BF16)
    b = _dot(tri_b, hi) + _dot(tri_b, mid) + _dot(tri_b, lo)
    last = (lambda s: s * C + C - 1) if fwd else (lambda s: s * C)
    tot = [b[last(s):last(s) + 1, :] for s in range(GLA_SUB)]
    zero = jnp.zeros_like(tot[0])
    before = [zero, tot[0], tot[0] + tot[1], tot[0] + tot[1] + tot[2]]
    after = [tot[1] + tot[2] + tot[3], tot[2] + tot[3], tot[3], zero]
    lead, trail = (before, after) if fwd else (after, before)

    q_t = q * jnp.exp(b) * (GLA_DK ** -0.5)
    k_t = (k * jnp.exp(-b)).astype(BF16)
    k_hat = k * jnp.exp(_rows(tot) - b)
    q_in = (q_t * _rows([jnp.exp(x) for x in lead])).astype(BF16)
    k_out = (k_hat * _rows([jnp.exp(x) for x in trail])).astype(BF16)
    k_hat = k_hat.astype(BF16)

    e1, e2, e12 = jnp.exp(tot[1]), jnp.exp(tot[2]), jnp.exp(tot[1] + tot[2])
    if fwd:
        q1, q2, q3 = q_t[C:], q_t[2 * C:], q_t[3 * C:]
    else:
        q1, q2, q3 = q_t[:3 * C], q_t[:2 * C], q_t[:C]
    q2 = q2 * jnp.concatenate([jnp.broadcast_to(e1, (C, GLA_DK)), jnp.broadcast_to(e2, (C, GLA_DK))], axis=0)
    q3 = q3 * e12
    y = _nt_dot(jnp.concatenate([q1, q2, q3], axis=0).astype(BF16), k_hat)
    y1, y2, y3 = y[:3 * C], y[3 * C:5 * C], y[5 * C:]
    z = lambda rows: jnp.zeros((rows, GLA_BLOCK), F32)
    if fwd:
        y1, y2, y3 = (jnp.concatenate([z(C), y1], 0), jnp.concatenate([z(2 * C), y2], 0),
                      jnp.concatenate([z(3 * C), y3], 0))
    else:
        y1, y2, y3 = (jnp.concatenate([y1, z(C)], 0), jnp.concatenate([y2, z(2 * C)], 0),
                      jnp.concatenate([y3, z(3 * C)], 0))
    x0 = _nt_dot(q_t.astype(BF16), k_t)
    a = jnp.where(tri, x0, jnp.where(dist == 1, y1, jnp.where(dist == 2, y2, jnp.where(dist == 3, y3, 0.0))))
    o = _dot(a.astype(BF16), v) + _nt_dot(q_in, st.astype(BF16))
    st_new = st * jnp.exp(tot[0] + tot[1] + tot[2] + tot[3]) + _tn_dot(v, k_out)
    return o, st_new


def _gla_dir_kernel(*refs, L, fwd):
    if fwd:
        q_ref, k_ref, v_ref, r_ref, w_ref, b_ref, o_ref, st_ref = refs
    else:
        q_ref, k_ref, v_ref, r_ref, w_ref, b_ref, g_ref, of_ref, gn_ref, o_ref, st_ref = refs
    nb = L // GLA_BLOCK

    @pl.when(pl.program_id(1) == 0)
    def _():
        st_ref[...] = jnp.zeros_like(st_ref)

    tri, dist = _gla_masks(fwd)

    def body(i, carry):
        blk = i if fwd else nb - 1 - i
        sl = pl.ds(pl.multiple_of(blk * GLA_BLOCK, GLA_BLOCK), GLA_BLOCK)
        r = r_ref[0, sl, :].astype(BF16)
        for h in range(GLA_HEADS):
            ks = slice(h * GLA_DK, (h + 1) * GLA_DK)
            vs = slice(h * GLA_DV, (h + 1) * GLA_DV)
            o, st = _gla_block(q_ref[0, sl, ks], k_ref[0, sl, ks], v_ref[0, sl, vs].astype(BF16), r,
                               w_ref[:, ks], b_ref[:, ks], st_ref[h], tri, dist, fwd)
            st_ref[h] = st
            if fwd:
                o_ref[0, sl, vs] = o
            else:
                o = o + of_ref[0, sl, vs]
                ms = jnp.mean(o * o, axis=-1, keepdims=True)
                o = o * lax.rsqrt(ms + EPS) * gn_ref[...]
                g = g_ref[0, sl, vs]
                o_ref[0, sl, vs] = (o * (g * jax.nn.sigmoid(g))).astype(o_ref.dtype)
        return carry

    lax.fori_loop(0, nb, body, 0)


def _gla_dir(proj3, r3, w, bias, fwd, o_f=None, gn=None, *, L=512):
    B, S, _ = proj3.shape
    nl = S // L
    pos = (lambda n: n) if fwd else (lambda n: nl - 1 - n)
    full = lambda b, n: (0, 0)
    in_specs = [
        pl.BlockSpec((1, L, GLA_QK), lambda b, n: (b, pos(n), COL_GQ * LANE // GLA_QK)),
        pl.BlockSpec((1, L, GLA_QK), lambda b, n: (b, pos(n), COL_GK * LANE // GLA_QK)),
        pl.BlockSpec((1, L, GLA_V), lambda b, n: (b, pos(n), COL_GV * LANE // GLA_V)),
        pl.BlockSpec((1, L, LANE), lambda b, n: (b, pos(n), 0)),
        pl.BlockSpec((LANE, GLA_QK), full),
        pl.BlockSpec((1, GLA_QK), full),
    ]
    args = [proj3, proj3, proj3, r3, w, bias]
    if not fwd:
        in_specs += [
            pl.BlockSpec((1, L, GLA_V), lambda b, n: (b, pos(n), COL_GG * LANE // GLA_V)),
            pl.BlockSpec((1, L, GLA_V), lambda b, n: (b, pos(n), 0)),
            pl.BlockSpec((1, GLA_DV), full),
        ]
        args += [proj3, o_f, gn]
    return pl.pallas_call(
        functools.partial(_gla_dir_kernel, L=L, fwd=fwd),
        grid=(B, nl),
        in_specs=in_specs,
        out_specs=pl.BlockSpec((1, L, GLA_V), lambda b, n: (b, pos(n), 0)),
        out_shape=jax.ShapeDtypeStruct((B, S, GLA_V), F32 if fwd else BF16),
        scratch_shapes=[pltpu.VMEM((GLA_HEADS, GLA_DV, GLA_DK), F32)],
        compiler_params=_params("parallel", "arbitrary"),
        name="gla_fwd" if fwd else "gla_bwd",
    )(*args)


def _rope_tables(S):
    rows = S // GRID_W
    pos_r = np.repeat(np.arange(rows), GRID_W).astype(np.float32)
    pos_c = np.tile(np.arange(GRID_W), rows).astype(np.float32)
    half = ATT_HD // 2
    inv = jnp.asarray(ROPE_THETA, F32) ** (-jnp.arange(0, half, 2, dtype=F32) / half)
    ang = jnp.concatenate([pos_r[:, None] * inv, pos_c[:, None] * inv], axis=-1)
    cos, sin = jnp.cos(ang), jnp.sin(ang)
    return jnp.concatenate([cos, cos], axis=-1), jnp.concatenate([-sin, sin], axis=-1)


def _qkprep_kernel(aq_ref, akv_ref, cos_ref, sin_ref, qn_ref, kn_ref, q_out, k_out, v_out):
    cos = cos_ref[...]
    sin = sin_ref[...]

    def norm_rope(x, w, scale):
        ms = jnp.mean(x * x, axis=-1, keepdims=True)
        y = x * lax.rsqrt(ms + EPS) * w
        return (y * cos + pltpu.roll(y, ATT_HD // 2, 1) * sin) * scale

    for h in range(ATT_HEADS):
        sl = slice(h * ATT_HD, (h + 1) * ATT_HD)
        q_out[:, sl] = norm_rope(aq_ref[:, sl], qn_ref[...], ATT_HD ** -0.5).astype(BF16)
    for h in range(ATT_KV_HEADS):
        sl = slice(h * ATT_HD, (h + 1) * ATT_HD)
        k_out[:, sl] = norm_rope(akv_ref[:, sl], kn_ref[...], 1.0).astype(BF16)
    v_out[...] = akv_ref[:, ATT_KV:2 * ATT_KV].astype(BF16)


def _qkprep(proj, cos, sin, qn, kn, S, *, tm=512):
    T = proj.shape[0]
    ns = S // tm
    return pl.pallas_call(
        _qkprep_kernel,
        grid=(T // tm,),
        in_specs=[
            pl.BlockSpec((tm, ATT_Q), lambda i: (i, COL_AQ * LANE // ATT_Q)),
            pl.BlockSpec((tm, 2 * ATT_KV), lambda i: (i, COL_AK * LANE // (2 * ATT_KV))),
            pl.BlockSpec((tm, ATT_HD), lambda i: (i % ns, 0)),
            pl.BlockSpec((tm, ATT_HD), lambda i: (i % ns, 0)),
            pl.BlockSpec((1, ATT_HD), lambda i: (0, 0)),
            pl.BlockSpec((1, ATT_HD), lambda i: (0, 0)),
        ],
        out_specs=[
            pl.BlockSpec((tm, ATT_Q), lambda i: (i, 0)),
            pl.BlockSpec((tm, ATT_KV), lambda i: (i, 0)),
            pl.BlockSpec((tm, ATT_KV), lambda i: (i, 0)),
        ],
        out_shape=[
            jax.ShapeDtypeStruct((T, ATT_Q), BF16),
            jax.ShapeDtypeStruct((T, ATT_KV), BF16),
            jax.ShapeDtypeStruct((T, ATT_KV), BF16),
        ],
        compiler_params=_params("parallel"),
        name="qkprep",
    )(proj, proj, cos, sin, qn, kn)


def _flash_kernel(q_ref, k_ref, v_ref, o_ref, qs_ref, m_ref, l_ref, acc_ref, *, tq):
    ki = pl.program_id(3)

    @pl.when(ki == 0)
    def _():
        for g in range(ATT_GROUP):
            qs_ref[g * tq:(g + 1) * tq, :] = q_ref[0, :, g * ATT_HD:(g + 1) * ATT_HD]
        m_ref[...] = jnp.full_like(m_ref, -jnp.inf)
        l_ref[...] = jnp.zeros_like(l_ref)
        acc_ref[...] = jnp.zeros_like(acc_ref)

    s = _nt_dot(qs_ref[...], k_ref[0])
    m_prev = m_ref[...]
    m_new = jnp.maximum(m_prev, jnp.max(s, axis=-1, keepdims=True))
    alpha = jnp.exp(m_prev - m_new)
    p = jnp.exp(s - m_new)
    l_ref[...] = alpha * l_ref[...] + jnp.sum(p, axis=-1, keepdims=True)
    acc_ref[...] = alpha * acc_ref[...] + _dot(p.astype(BF16), v_ref[0])
    m_ref[...] = m_new

    @pl.when(ki == pl.num_programs(3) - 1)
    def _():
        out = acc_ref[...] / l_ref[...]
        for g in range(ATT_GROUP):
            for u in range(tq // Q_BLOCK):
                rows = slice(g * tq + u * Q_BLOCK, g * tq + (u + 1) * Q_BLOCK)
                o_ref[0, g, :, u * ATT_HD:(u + 1) * ATT_HD] = out[rows, :].astype(o_ref.dtype)


def _flash(q3, k3, v3, *, tq=256, tk=1024):
    B, S, _ = q3.shape
    gw = ATT_GROUP * ATT_HD
    n_qt = S // tq
    return pl.pallas_call(
        functools.partial(_flash_kernel, tq=tq),
        grid=(B, ATT_KV_HEADS, S // tq, S // tk),
        in_specs=[
            pl.BlockSpec((1, tq, gw), lambda b, h, qi, ki: (b, qi, h)),
            pl.BlockSpec((1, tk, ATT_HD), lambda b, h, qi, ki: (b, ki, h)),
            pl.BlockSpec((1, tk, ATT_HD), lambda b, h, qi, ki: (b, ki, h)),
        ],
        out_specs=pl.BlockSpec((1, ATT_GROUP, Q_BLOCK, tq), lambda b, h, qi, ki: (b, 0, 0, h * n_qt + qi)),
        out_shape=jax.ShapeDtypeStruct((B, ATT_GROUP, Q_BLOCK, ATT_KV_HEADS * S), BF16),
        scratch_shapes=[
            pltpu.VMEM((ATT_GROUP * tq, ATT_HD), BF16),
            pltpu.VMEM((ATT_GROUP * tq, 1), F32),
            pltpu.VMEM((ATT_GROUP * tq, 1), F32),
            pltpu.VMEM((ATT_GROUP * tq, ATT_HD), F32),
        ],
        compiler_params=_params("parallel", "parallel", "parallel", "arbitrary"),
        name="flash",
    )(q3, k3, v3)


def _merge_kernel(x_ref, oa_ref, ob_ref, za_ref, zb_ref, woa_ref, wob_ref, wout_ref, nf_ref, wrt_ref,
                  x1_ref, h2_ref, aff_ref):
    ya = _dot(oa_ref[...], woa_ref[...])
    yb = _dot(ob_ref[...], wob_ref[...])
    merged = jax.nn.sigmoid(za_ref[...]) * ya + jax.nn.sigmoid(zb_ref[...]) * yb
    x1 = x_ref[...] + _dot(merged.astype(BF16), wout_ref[...])
    x1_ref[...] = x1
    ms = jnp.mean(x1 * x1, axis=-1, keepdims=True)
    h2 = x1 * lax.rsqrt(ms + EPS) * nf_ref[...]
    h2_ref[...] = h2.astype(BF16)
    logits = _nt_dot(wrt_ref[...], h2, precision=lax.Precision.HIGHEST)
    e = jnp.exp(logits - jnp.max(logits, axis=0, keepdims=True))
    aff_ref[...] = e / jnp.sum(e, axis=0, keepdims=True)


def _merge(x, o_a, o_b, proj, w_oa, w_ob, w_out, nf, w_rt, *, tm=512):
    T = x.shape[0]
    full = lambda i: (0, 0)
    return pl.pallas_call(
        _merge_kernel,
        grid=(T // tm,),
        in_specs=[
            pl.BlockSpec((tm, D_MODEL), lambda i: (i, 0)),
            pl.BlockSpec((tm, GLA_V), lambda i: (i, 0)),
            pl.BlockSpec((tm, ATT_Q), lambda i: (i, 0)),
            pl.BlockSpec((tm, D_MODEL), lambda i: (i, COL_ZA * LANE // D_MODEL)),
            pl.BlockSpec((tm, D_MODEL), lambda i: (i, COL_ZB * LANE // D_MODEL)),
            pl.BlockSpec((GLA_V, D_MODEL), full),
            pl.BlockSpec((ATT_Q, D_MODEL), full),
            pl.BlockSpec((D_MODEL, D_MODEL), full),
            pl.BlockSpec((1, D_MODEL), full),
            pl.BlockSpec((N_EXPERTS, D_MODEL), full),
        ],
        out_specs=[
            pl.BlockSpec((tm, D_MODEL), lambda i: (i, 0)),
            pl.BlockSpec((tm, D_MODEL), lambda i: (i, 0)),
            pl.BlockSpec((N_EXPERTS, tm), lambda i: (0, i)),
        ],
        out_shape=[
            jax.ShapeDtypeStruct((T, D_MODEL), F32),
            jax.ShapeDtypeStruct((T, D_MODEL), BF16),
            jax.ShapeDtypeStruct((N_EXPERTS, T), F32),
        ],
        compiler_params=_params("parallel"),
        name="merge",
    )(x, o_a, o_b, proj, proj, w_oa, w_ob, w_out, nf, w_rt)


def _lane_cumsum(x):
    n = x.shape[1]
    lane = lax.broadcasted_iota(I32, x.shape, 1)
    shift = 1
    while shift < n:
        x = x + jnp.where(lane >= shift, pltpu.roll(x, shift, 1), 0)
        shift *= 2
    return x


def _select_kernel(aff_ref, slot_ref, incl_ref, *, cap):
    bits = pltpu.bitcast(aff_ref[...], I32)

    def body(i, thr):
        cand = thr | jnp.left_shift(jnp.int32(1), 30 - i)
        cnt = jnp.sum((bits >= cand).astype(I32), axis=1, keepdims=True)
        return jnp.where(cnt >= cap, cand, thr)

    thr = lax.fori_loop(0, 31, body, jnp.zeros((bits.shape[0], 1), I32))
    above = bits > thr
    need = cap - jnp.sum(above.astype(I32), axis=1, keepdims=True)
    ties = bits == thr
    sel = above | (ties & (_lane_cumsum(ties.astype(I32)) <= need))
    incl = _lane_cumsum(sel.astype(I32))
    slot_ref[...] = jnp.where(sel, incl - 1, -1)
    incl_ref[...] = incl


def _select(aff_t, cap):
    E, T = aff_t.shape
    full = lambda: (0, 0)
    slot, incl = pl.pallas_call(
        functools.partial(_select_kernel, cap=cap),
        in_specs=[pl.BlockSpec((E, T), full)],
        out_specs=[pl.BlockSpec((E, T), full), pl.BlockSpec((E, T), full)],
        out_shape=[jax.ShapeDtypeStruct((E, T), I32), jax.ShapeDtypeStruct((E, T), I32)],
        compiler_params=pltpu.CompilerParams(vmem_limit_bytes=VMEM_LIMIT),
        name="select",
    )(aff_t)
    offs = jnp.concatenate([jnp.zeros((E, 1), I32), incl[:, ROUTE_TB - 1::ROUTE_TB]], axis=1)
    return slot, offs


def _gather_kernel(offs_ref, slot_ref, gate_ref, h_ref, xe_ref, gc_ref):
    e = pl.program_id(0)
    t = pl.program_id(1)

    @pl.when(t == 0)
    def _():
        xe_ref[...] = jnp.zeros_like(xe_ref)
        gc_ref[...] = jnp.zeros_like(gc_ref)

    off = offs_ref[e, t]
    end = offs_ref[e, t + 1]
    slot_row = slot_ref[0]
    gate_row = gate_ref[0]
    h = h_ref[...]
    base0 = (off // ROUTE_WIN) * ROUTE_WIN
    srow = lax.broadcasted_iota(I32, (ROUTE_WIN, ROUTE_TB), 0)
    for w in range(ROUTE_NWIN):
        base = base0 + w * ROUTE_WIN

        @pl.when(base < end)
        def _():
            hit = slot_row == (srow + base)
            rows = pl.ds(pl.multiple_of(base, ROUTE_WIN), ROUTE_WIN)
            picked = _dot(jnp.where(hit, 1.0, 0.0).astype(BF16), h)
            xe_ref[0, rows, :] += picked.astype(BF16)
            gc_ref[0, rows, :] += jnp.sum(jnp.where(hit, gate_row, 0.0), axis=1, keepdims=True)


def _gather(offs, slot3, aff3, h2, cap):
    E, _, T = slot3.shape
    assert cap % ROUTE_WIN == 0 and T % ROUTE_TB == 0
    row_spec = pl.BlockSpec((1, 1, ROUTE_TB), lambda e, t, offs: (e, 0, t))
    return pl.pallas_call(
        _gather_kernel,
        grid_spec=pltpu.PrefetchScalarGridSpec(
            num_scalar_prefetch=1,
            grid=(E, T // ROUTE_TB),
            in_specs=[row_spec, row_spec, pl.BlockSpec((ROUTE_TB, D_MODEL), lambda e, t, offs: (t, 0))],
            out_specs=[
                pl.BlockSpec((1, cap, D_MODEL), lambda e, t, offs: (e, 0, 0)),
                pl.BlockSpec((1, cap, 1), lambda e, t, offs: (e, 0, 0)),
            ],
        ),
        out_shape=[
            jax.ShapeDtypeStruct((E, cap, D_MODEL), BF16),
            jax.ShapeDtypeStruct((E, cap, 1), F32),
        ],
        compiler_params=_params("parallel", "arbitrary"),
        name="gather",
    )(offs, slot3, aff3, h2)


def _scatter_kernel(offs_ref, slot_ref, x1_ref, *refs, nblk):
    y_refs, o_ref = refs[:ROUTE_NWIN], refs[ROUTE_NWIN]
    e = pl.program_id(1)
    ti = pl.program_id(2)
    t = pl.program_id(0) * nblk + ti
    off = offs_ref[e, t]
    end = offs_ref[e, t + 1]
    rows = pl.ds(pl.multiple_of(ti * ROUTE_TB, ROUTE_TB), ROUTE_TB)

    @pl.when(e == 0)
    def _():
        o_ref[rows, :] = x1_ref[rows, :]

    slot_row = slot_ref[0]
    base0 = (off // ROUTE_WIN) * ROUTE_WIN
    srow = lax.broadcasted_iota(I32, (ROUTE_WIN, ROUTE_TB), 0)
    for w in range(ROUTE_NWIN):
        base = base0 + w * ROUTE_WIN

        @pl.when(base < end)
        def _():
            hit = slot_row == (srow + base)
            o_ref[rows, :] += _tn_dot(jnp.where(hit, 1.0, 0.0).astype(BF16), y_refs[w][0])


def _scatter(offs, slot3, x1, ye, *, sb=2048):
    E, _, T = slot3.shape
    cap = ye.shape[1]
    sb = min(sb, T)
    nblk = sb // ROUTE_TB
    nwin = cap // ROUTE_WIN

    def y_spec(w):
        def index(s, e, ti, offs):
            return (e, jnp.minimum(offs[e, s * nblk + ti] // ROUTE_WIN + w, nwin - 1), 0)
        return pl.BlockSpec((1, ROUTE_WIN, D_MODEL), index)

    return pl.pallas_call(
        functools.partial(_scatter_kernel, nblk=nblk),
        grid_spec=pltpu.PrefetchScalarGridSpec(
            num_scalar_prefetch=1,
            grid=(T // sb, E, nblk),
            in_specs=[
                pl.BlockSpec((1, 1, ROUTE_TB), lambda s, e, ti, offs: (e, 0, s * nblk + ti)),
                pl.BlockSpec((sb, D_MODEL), lambda s, e, ti, offs: (s, 0)),
            ] + [y_spec(w) for w in range(ROUTE_NWIN)],
            out_specs=pl.BlockSpec((sb, D_MODEL), lambda s, e, ti, offs: (s, 0)),
        ),
        out_shape=jax.ShapeDtypeStruct((T, D_MODEL), F32),
        compiler_params=_params("parallel", "arbitrary", "arbitrary"),
        name="scatter",
    )(offs, slot3, x1, *([ye] * ROUTE_NWIN))


def _ffn_kernel(xe_ref, gate_ref, wg_ref, wu_ref, wd_ref, o_ref, acc_ref, *, rows):
    f = pl.program_id(1)
    wg = wg_ref[0].astype(BF16)
    wu = wu_ref[0].astype(BF16)
    wd = wd_ref[0].astype(BF16)
    cap = xe_ref.shape[1]

    def body(i, carry):
        sl = pl.ds(pl.multiple_of(i * rows, rows), rows)
        xe = xe_ref[0, sl, :]
        hg = _dot(xe, wg)
        hu = _dot(xe, wu)
        hid = (hg * jax.nn.sigmoid(hg) * hu).astype(BF16)
        y = _dot(hid, wd)

        @pl.when(f == 0)
        def _():
            acc_ref[sl, :] = y

        @pl.when(f != 0)
        def _():
            acc_ref[sl, :] += y
        return carry

    lax.fori_loop(0, cap // rows, body, 0)

    @pl.when(f == pl.num_programs(1) - 1)
    def _():
        o_ref[0] = (acc_ref[...] * gate_ref[0]).astype(o_ref.dtype)


def _ffn(xe, gate, w_g, w_u, w_d, *, tf=512, rows=256):
    E, cap, _ = xe.shape
    rows = min(rows, cap)
    return pl.pallas_call(
        functools.partial(_ffn_kernel, rows=rows),
        grid=(E, EXPERT_FF // tf),
        in_specs=[
            pl.BlockSpec((1, cap, D_MODEL), lambda e, f: (e, 0, 0)),
            pl.BlockSpec((1, cap, 1), lambda e, f: (e, 0, 0)),
            pl.BlockSpec((1, D_MODEL, tf), lambda e, f: (e, 0, f)),
            pl.BlockSpec((1, D_MODEL, tf), lambda e, f: (e, 0, f)),
            pl.BlockSpec((1, tf, D_MODEL), lambda e, f: (e, f, 0)),
        ],
        out_specs=pl.BlockSpec((1, cap, D_MODEL), lambda e, f: (e, 0, 0)),
        out_shape=jax.ShapeDtypeStruct((E, cap, D_MODEL), BF16),
        scratch_shapes=[pltpu.VMEM((cap, D_MODEL), F32)],
        compiler_params=_params("parallel", "arbitrary"),
        name="ffn",
    )(xe, gate, w_g, w_u, w_d)


def _prep_weights(w_in, w_gk_f, w_gk_b, q_norm, k_norm):
    pts = np.cumsum([GLA_QK, GLA_QK, GLA_V, GLA_V, GLA_RANK, GLA_RANK, ATT_Q, ATT_KV, ATT_KV, D_MODEL])
    gq, gk, gv, gg, r_f, r_b, aq, ak, av, z_a, z_b = jnp.split(w_in, pts, axis=-1)
    perm = np.concatenate([np.arange(0, ATT_HD, 2), np.arange(1, ATT_HD, 2)])
    perm_q = (np.arange(ATT_HEADS)[:, None] * ATT_HD + perm[None, :]).reshape(-1)
    perm_k = (np.arange(ATT_KV_HEADS)[:, None] * ATT_HD + perm[None, :]).reshape(-1)
    w_main = jnp.concatenate([gq, gk, gv, gg, z_a, z_b, aq[:, perm_q], ak[:, perm_k], av], axis=-1)
    w_r = jnp.pad(jnp.concatenate([r_f, r_b], axis=-1), ((0, 0), (0, LANE - 2 * GLA_RANK)))
    wf = jnp.pad(w_gk_f, ((0, LANE - GLA_RANK), (0, 0)))
    wb = jnp.pad(w_gk_b, ((GLA_RANK, LANE - 2 * GLA_RANK), (0, 0)))
    return (w_main.astype(BF16), w_r.astype(BF16), wf.astype(BF16), wb.astype(BF16),
            q_norm[perm][None, :], k_norm[perm][None, :])


def _layer(x3, norm_mix, w_main, w_r, wf, bf, wb, bb, gla_norm, qn, kn,
           w_oa, w_ob, w_out, norm_ffn, w_rt, w_e_gate, w_e_up, w_e_down):
    B, S, D = x3.shape
    T = B * S
    x = x3.reshape(T, D)
    proj, r = _inproj(x, norm_mix, w_r, w_main)
    proj3, r3 = proj.reshape(B, S, -1), r.reshape(B, S, LANE)
    o_f = _gla_dir(proj3, r3, wf, bf, True)
    o_a = _gla_dir(proj3, r3, wb, bb, False, o_f, gla_norm)
    cos, sin = _rope_tables(S)
    q, k, v = _qkprep(proj, cos, sin, qn, kn, S)
    o_b = _flash(q.reshape(B, S, ATT_Q), k.reshape(B, S, ATT_KV), v.reshape(B, S, ATT_KV))
    x1, h2, aff_t = _merge(x, o_a.reshape(T, GLA_V), o_b.reshape(T, ATT_Q), proj,
                           w_oa, w_ob, w_out, norm_ffn, w_rt)
    cap = CAPACITY_FACTOR * T // N_EXPERTS
    slot, offs = _select(aff_t, cap)
    slot3 = slot.reshape(N_EXPERTS, 1, T)
    xe, gate = _gather(offs, slot3, aff_t.reshape(N_EXPERTS, 1, T), h2, cap)
    ye = _ffn(xe, gate, w_e_gate, w_e_up, w_e_down)
    out = _scatter(offs, slot3, x1, ye)
    return out.reshape(B, S, D)


def kernel(x_prompt, x_sample, norm_mix, w_in, w_gk_f, b_gk_f, w_gk_b, b_gk_b, gla_norm, q_norm, k_norm,
           w_o_gla, w_o_att, w_out, norm_ffn, w_router, w_e_gate, w_e_up, w_e_down):
    w_main, w_r, wf, wb, qn, kn = _prep_weights(w_in[0], w_gk_f[0], w_gk_b[0], q_norm[0], k_norm[0])
    args = (norm_mix[0][None, :], w_main, w_r, wf, b_gk_f[0][None, :], wb, b_gk_b[0][None, :],
            gla_norm[0][None, :], qn, kn,
            w_o_gla[0].astype(BF16), w_o_att[0].astype(BF16), w_out[0].astype(BF16),
            norm_ffn[0][None, :], w_router[0].T, w_e_gate[0], w_e_up[0], w_e_down[0])
    return (_layer(x_prompt, *args), _layer(x_sample, *args))
```

```python
import functools

import jax
import jax.numpy as jnp
import numpy as np
from jax import lax
from jax.experimental import pallas as pl
from jax.experimental.pallas import tpu as pltpu

D_MODEL = 1024
GRID_W = 64
GLA_HEADS = 4
GLA_DK = 128
GLA_DV = 256
GLA_RANK = 16
GLA_GATE_NORM = 16.0
GLA_CHUNK = 64
GLA_QK = GLA_HEADS * GLA_DK
GLA_V = GLA_HEADS * GLA_DV
ATT_HEADS = 8
ATT_KV_HEADS = 2
ATT_GROUP = ATT_HEADS // ATT_KV_HEADS
ATT_HD = 128
ATT_Q = ATT_HEADS * ATT_HD
ATT_KV = ATT_KV_HEADS * ATT_HD
ROPE_THETA = 10000.0
Q_BLOCK = 128
N_EXPERTS = 16
CAPACITY_FACTOR = 2
EXPERT_FF = 2048
EPS = 1e-6
LOG2_E = 1.4426950408889634

LANE = 128
VMEM_LIMIT = 56 * 1024 * 1024

BF16 = jnp.bfloat16
F32 = jnp.float32
I32 = jnp.int32

COL_GQ, COL_GK, COL_GV, COL_GG = 0, 4, 8, 16
COL_ZA, COL_ZB, COL_AQ, COL_AK, COL_AV = 24, 32, 40, 48, 50
N_MAIN_COLS = 52 * LANE

GLA_BLOCK = 256
GLA_SUB = GLA_BLOCK // GLA_CHUNK
ROUTE_TB = 256
ROUTE_WIN = 128
ROUTE_NWIN = ROUTE_TB // ROUTE_WIN + 1
GATHER_TC = 1024


def _nt_dot(a, b, **kw):
    return lax.dot_general(a, b, (((1,), (1,)), ((), ())), preferred_element_type=F32, **kw)


def _tn_dot(a, b):
    return lax.dot_general(a, b, (((0,), (0,)), ((), ())), preferred_element_type=F32)


def _dot(a, b):
    return jnp.dot(a, b, preferred_element_type=F32)


def _params(*sem):
    return pltpu.CompilerParams(dimension_semantics=sem, vmem_limit_bytes=VMEM_LIMIT)


def _inproj_kernel(x_ref, g_ref, wr_ref, w_ref, o_ref, r_ref, hn_ref):
    @pl.when(pl.program_id(1) == 0)
    def _():
        x = x_ref[...]
        ms = jnp.mean(x * x, axis=-1, keepdims=True)
        hn = (x * lax.rsqrt(ms + EPS) * g_ref[...]).astype(BF16)
        hn_ref[...] = hn
        r_ref[...] = _dot(hn, wr_ref[...])

    o_ref[...] = _dot(hn_ref[...], w_ref[...])


def _inproj(x, g, w_r, w_main, *, tm=1024, n_col_tiles=4):
    T = x.shape[0]
    tn = N_MAIN_COLS // n_col_tiles
    return pl.pallas_call(
        _inproj_kernel,
        grid=(T // tm, n_col_tiles),
        in_specs=[
            pl.BlockSpec((tm, D_MODEL), lambda i, j: (i, 0)),
            pl.BlockSpec((1, D_MODEL), lambda i, j: (0, 0)),
            pl.BlockSpec((D_MODEL, LANE), lambda i, j: (0, 0)),
            pl.BlockSpec((D_MODEL, tn), lambda i, j: (0, j)),
        ],
        out_specs=[
            pl.BlockSpec((tm, tn), lambda i, j: (i, j)),
            pl.BlockSpec((tm, LANE), lambda i, j: (i, 0)),
        ],
        out_shape=[
            jax.ShapeDtypeStruct((T, N_MAIN_COLS), F32),
            jax.ShapeDtypeStruct((T, LANE), F32),
        ],
        scratch_shapes=[pltpu.VMEM((tm, D_MODEL), BF16)],
        compiler_params=_params("parallel", "arbitrary"),
        name="inproj",
    )(x, g, w_r, w_main)


def _log_sigmoid(x):
    return jnp.minimum(x, 0.0) - jnp.log1p(jnp.exp(-jnp.abs(x)))


def _rows(parts):
    return jnp.concatenate([jnp.broadcast_to(p, (GLA_CHUNK, p.shape[-1])) for p in parts], axis=0)


def _gla_masks(fwd):
    n = GLA_BLOCK
    row = lax.broadcasted_iota(I32, (n, n), 0)
    col = lax.broadcasted_iota(I32, (n, n), 1)
    same = (row // GLA_CHUNK) == (col // GLA_CHUNK)
    tri = same & ((col <= row) if fwd else (col >= row))
    dist = (row // GLA_CHUNK - col // GLA_CHUNK) * (1 if fwd else -1)
    return tri, dist


def _gla_block(q, k, v, r, w, bias, st, tri, dist, fwd):
    C = GLA_CHUNK
    lg = _log_sigmoid(_dot(r, w) + bias) * (1.0 / GLA_GATE_NORM)
    hi = lg.astype(BF16)
    rem = lg - hi.astype(F32)
    mid = rem.astype(BF16)
    lo = (rem - mid.astype(F32)).astype(BF16)
    tri_b = jnp.where(tri, 1.0, 0.0).astype(BF16)
    b = _dot(tri_b, hi) + _dot(tri_b, mid) + _dot(tri_b, lo)
    last = (lambda s: s * C + C - 1) if fwd else (lambda s: s * C)
    tot = [b[last(s):last(s) + 1, :] for s in range(GLA_SUB)]
    zero = jnp.zeros_like(tot[0])
    before = [zero, tot[0], tot[0] + tot[1], tot[0] + tot[1] + tot[2]]
    after = [tot[1] + tot[2] + tot[3], tot[2] + tot[3], tot[3], zero]
    lead, trail = (before, after) if fwd else (after, before)

    q_t = q * jnp.exp(b) * (GLA_DK ** -0.5)
    k_t = (k * jnp.exp(-b)).astype(BF16)
    k_hat = k * jnp.exp(_rows(tot) - b)
    q_in = (q_t * _rows([jnp.exp(x) for x in lead])).astype(BF16)
    k_out = (k_hat * _rows([jnp.exp(x) for x in trail])).astype(BF16)
    k_hat = k_hat.astype(BF16)

    e1, e2, e12 = jnp.exp(tot[1]), jnp.exp(tot[2]), jnp.exp(tot[1] + tot[2])
    if fwd:
        q1, q2, q3 = q_t[C:], q_t[2 * C:], q_t[3 * C:]
    else:
        q1, q2, q3 = q_t[:3 * C], q_t[:2 * C], q_t[:C]
    q2 = q2 * jnp.concatenate([jnp.broadcast_to(e1, (C, GLA_DK)), jnp.broadcast_to(e2, (C, GLA_DK))], axis=0)
    q3 = q3 * e12
    y = _nt_dot(jnp.concatenate([q1, q2, q3], axis=0).astype(BF16), k_hat)
    y1, y2, y3 = y[:3 * C], y[3 * C:5 * C], y[5 * C:]
    z = lambda rows: jnp.zeros((rows, GLA_BLOCK), F32)
    if fwd:
        y1, y2, y3 = (jnp.concatenate([z(C), y1], 0), jnp.concatenate([z(2 * C), y2], 0),
                      jnp.concatenate([z(3 * C), y3], 0))
    else:
        y1, y2, y3 = (jnp.concatenate([y1, z(C)], 0), jnp.concatenate([y2, z(2 * C)], 0),
                      jnp.concatenate([y3, z(3 * C)], 0))
    x0 = _nt_dot(q_t.astype(BF16), k_t)
    a = jnp.where(tri, x0, jnp.where(dist == 1, y1, jnp.where(dist == 2, y2, jnp.where(dist == 3, y3, 0.0))))
    o = _dot(a.astype(BF16), v) + _nt_dot(q_in, st.astype(BF16))
    st_new = st * jnp.exp(tot[0] + tot[1] + tot[2] + tot[3]) + _tn_dot(v, k_out)
    return o, st_new


def _gla_dir_kernel(*refs, L, fwd):
    if fwd:
        q_ref, k_ref, v_ref, r_ref, w_ref, b_ref, o_ref, st_ref = refs
    else:
        q_ref, k_ref, v_ref, r_ref, w_ref, b_ref, g_ref, of_ref, gn_ref, o_ref, st_ref = refs
    nb = L // GLA_BLOCK

    @pl.when(pl.program_id(1) == 0)
    def _():
        st_ref[...] = jnp.zeros_like(st_ref)

    tri, dist = _gla_masks(fwd)

    def body(i, carry):
        blk = i if fwd else nb - 1 - i
        sl = pl.ds(pl.multiple_of(blk * GLA_BLOCK, GLA_BLOCK), GLA_BLOCK)
        r = r_ref[0, sl, :].astype(BF16)
        for h in range(GLA_HEADS):
            ks = slice(h * GLA_DK, (h + 1) * GLA_DK)
            vs = slice(h * GLA_DV, (h + 1) * GLA_DV)
            o, st = _gla_block(q_ref[0, sl, ks], k_ref[0, sl, ks], v_ref[0, sl, vs].astype(BF16), r,
                               w_ref[:, ks], b_ref[:, ks], st_ref[h], tri, dist, fwd)
            st_ref[h] = st
            if fwd:
                o_ref[0, sl, vs] = o
            else:
                o = o + of_ref[0, sl, vs]
                ms = jnp.mean(o * o, axis=-1, keepdims=True)
                o = o * lax.rsqrt(ms + EPS) * gn_ref[...]
                g = g_ref[0, sl, vs]
                o_ref[0, sl, vs] = (o * (g * jax.nn.sigmoid(g))).astype(o_ref.dtype)
        return carry

    lax.fori_loop(0, nb, body, 0)


def _gla_dir(proj3, r3, w, bias, fwd, o_f=None, gn=None, *, L=512):
    B, S, _ = proj3.shape
    nl = S // L
    pos = (lambda n: n) if fwd else (lambda n: nl - 1 - n)
    full = lambda b, n: (0, 0)
    in_specs = [
        pl.BlockSpec((1, L, GLA_QK), lambda b, n: (b, pos(n), COL_GQ * LANE // GLA_QK)),
        pl.BlockSpec((1, L, GLA_QK), lambda b, n: (b, pos(n), COL_GK * LANE // GLA_QK)),
        pl.BlockSpec((1, L, GLA_V), lambda b, n: (b, pos(n), COL_GV * LANE // GLA_V)),
        pl.BlockSpec((1, L, LANE), lambda b, n: (b, pos(n), 0)),
        pl.BlockSpec((LANE, GLA_QK), full),
        pl.BlockSpec((1, GLA_QK), full),
    ]
    args = [proj3, proj3, proj3, r3, w, bias]
    if not fwd:
        in_specs += [
            pl.BlockSpec((1, L, GLA_V), lambda b, n: (b, pos(n), COL_GG * LANE // GLA_V)),
            pl.BlockSpec((1, L, GLA_V), lambda b, n: (b, pos(n), 0)),
            pl.BlockSpec((1, GLA_DV), full),
        ]
        args += [proj3, o_f, gn]
    return pl.pallas_call(
        functools.partial(_gla_dir_kernel, L=L, fwd=fwd),
        grid=(B, nl),
        in_specs=in_specs,
        out_specs=pl.BlockSpec((1, L, GLA_V), lambda b, n: (b, pos(n), 0)),
        out_shape=jax.ShapeDtypeStruct((B, S, GLA_V), F32 if fwd else BF16),
        scratch_shapes=[pltpu.VMEM((GLA_HEADS, GLA_DV, GLA_DK), F32)],
        compiler_params=_params("parallel", "arbitrary"),
        name="gla_fwd" if fwd else "gla_bwd",
    )(*args)


def _rope_tables(S):
    rows = S // GRID_W
    pos_r = np.repeat(np.arange(rows), GRID_W).astype(np.float32)
    pos_c = np.tile(np.arange(GRID_W), rows).astype(np.float32)
    half = ATT_HD // 2
    inv = jnp.asarray(ROPE_THETA, F32) ** (-jnp.arange(0, half, 2, dtype=F32) / half)
    ang = jnp.concatenate([pos_r[:, None] * inv, pos_c[:, None] * inv], axis=-1)
    cos, sin = jnp.cos(ang), jnp.sin(ang)
    return jnp.concatenate([cos, cos], axis=-1), jnp.concatenate([-sin, sin], axis=-1)


def _qkprep_kernel(aq_ref, akv_ref, cos_ref, sin_ref, qn_ref, kn_ref, q_out, k_out, v_out):
    cos = cos_ref[...]
    sin = sin_ref[...]

    def norm_rope(x, w, scale):
        ms = jnp.mean(x * x, axis=-1, keepdims=True)
        y = x * lax.rsqrt(ms + EPS) * w
        return (y * cos + pltpu.roll(y, ATT_HD // 2, 1) * sin) * scale

    for h in range(ATT_HEADS):
        sl = slice(h * ATT_HD, (h + 1) * ATT_HD)
        q_out[:, sl] = norm_rope(aq_ref[:, sl], qn_ref[...], ATT_HD ** -0.5 * LOG2_E).astype(BF16)
    for h in range(ATT_KV_HEADS):
        sl = slice(h * ATT_HD, (h + 1) * ATT_HD)
        k_out[:, sl] = norm_rope(akv_ref[:, sl], kn_ref[...], 1.0).astype(BF16)
    for h in range(ATT_KV_HEADS):
        v_out[:, 2 * h * ATT_HD:(2 * h + 1) * ATT_HD] = akv_ref[:, ATT_KV + h * ATT_HD:ATT_KV + (h + 1) * ATT_HD].astype(BF16)
        v_out[:, (2 * h + 1) * ATT_HD:(2 * h + 2) * ATT_HD] = jnp.ones((v_out.shape[0], ATT_HD), BF16)


def _qkprep(proj, cos, sin, qn, kn, S, *, tm=512):
    T = proj.shape[0]
    ns = S // tm
    return pl.pallas_call(
        _qkprep_kernel,
        grid=(T // tm,),
        in_specs=[
            pl.BlockSpec((tm, ATT_Q), lambda i: (i, COL_AQ * LANE // ATT_Q)),
            pl.BlockSpec((tm, 2 * ATT_KV), lambda i: (i, COL_AK * LANE // (2 * ATT_KV))),
            pl.BlockSpec((tm, ATT_HD), lambda i: (i % ns, 0)),
            pl.BlockSpec((tm, ATT_HD), lambda i: (i % ns, 0)),
            pl.BlockSpec((1, ATT_HD), lambda i: (0, 0)),
            pl.BlockSpec((1, ATT_HD), lambda i: (0, 0)),
        ],
        out_specs=[
            pl.BlockSpec((tm, ATT_Q), lambda i: (i, 0)),
            pl.BlockSpec((tm, ATT_KV), lambda i: (i, 0)),
            pl.BlockSpec((tm, 2 * ATT_KV), lambda i: (i, 0)),
        ],
        out_shape=[
            jax.ShapeDtypeStruct((T, ATT_Q), BF16),
            jax.ShapeDtypeStruct((T, ATT_KV), BF16),
            jax.ShapeDtypeStruct((T, 2 * ATT_KV), BF16),
        ],
        compiler_params=_params("parallel"),
        name="qkprep",
    )(proj, proj, cos, sin, qn, kn)


def _flash_kernel(q_ref, k_ref, v_ref, o_ref, m_ref, acc_ref, *, tq):
    ki = pl.program_id(3)
    tk = k_ref.shape[1]

    @pl.when(ki == 0)
    def _():
        m_ref[...] = jnp.full_like(m_ref, -jnp.inf)
        acc_ref[...] = jnp.zeros_like(acc_ref)

    k = k_ref[0]
    v = v_ref[0]
    for g in range(ATT_GROUP):
        s = _nt_dot(q_ref[0, :, g * ATT_HD:(g + 1) * ATT_HD], k)
        part = s[:, :LANE]
        for j in range(1, tk // LANE):
            part = jnp.maximum(part, s[:, j * LANE:(j + 1) * LANE])
        m = m_ref[g]
        m_new = jnp.maximum(m, jnp.max(part, axis=-1, keepdims=True))
        p = jnp.exp2(s - m_new).astype(BF16)
        acc_ref[g] = jnp.exp2(m - m_new) * acc_ref[g] + _dot(p, v)
        m_ref[g] = m_new

    @pl.when(ki == pl.num_programs(3) - 1)
    def _():
        for g in range(ATT_GROUP):
            acc = acc_ref[g]
            out = acc[:, :ATT_HD] / acc[:, ATT_HD:ATT_HD + 1]
            for u in range(tq // Q_BLOCK):
                rows = slice(u * Q_BLOCK, (u + 1) * Q_BLOCK)
                o_ref[0, g, :, u * ATT_HD:(u + 1) * ATT_HD] = out[rows, :].astype(o_ref.dtype)


def _flash(q3, k3, v3, *, tq=512, tk=2048):
    B, S, _ = q3.shape
    gw = ATT_GROUP * ATT_HD
    n_qt = S // tq
    tk = min(tk, S)
    return pl.pallas_call(
        functools.partial(_flash_kernel, tq=tq),
        grid=(B, ATT_KV_HEADS, S // tq, S // tk),
        in_specs=[
            pl.BlockSpec((1, tq, gw), lambda b, h, qi, ki: (b, qi, h)),
            pl.BlockSpec((1, tk, ATT_HD), lambda b, h, qi, ki: (b, ki, h)),
            pl.BlockSpec((1, tk, 2 * ATT_HD), lambda b, h, qi, ki: (b, ki, h)),
        ],
        out_specs=pl.BlockSpec((1, ATT_GROUP, Q_BLOCK, tq), lambda b, h, qi, ki: (b, 0, 0, h * n_qt + qi)),
        out_shape=jax.ShapeDtypeStruct((B, ATT_GROUP, Q_BLOCK, ATT_KV_HEADS * S), BF16),
        scratch_shapes=[
            pltpu.VMEM((ATT_GROUP, tq, 1), F32),
            pltpu.VMEM((ATT_GROUP, tq, 2 * ATT_HD), F32),
        ],
        compiler_params=_params("parallel", "parallel", "parallel", "arbitrary"),
        name="flash",
    )(q3, k3, v3)


def _merge_kernel(x_ref, oa_ref, ob_ref, za_ref, zb_ref, woa_ref, wob_ref, wout_ref, nf_ref, wrt_ref,
                  x1_ref, h2_ref, aff_ref):
    ya = _dot(oa_ref[...], woa_ref[...])
    yb = _dot(ob_ref[...], wob_ref[...])
    merged = jax.nn.sigmoid(za_ref[...]) * ya + jax.nn.sigmoid(zb_ref[...]) * yb
    x1 = x_ref[...] + _dot(merged.astype(BF16), wout_ref[...])
    x1_ref[...] = x1
    ms = jnp.mean(x1 * x1, axis=-1, keepdims=True)
    h2 = x1 * lax.rsqrt(ms + EPS) * nf_ref[...]
    h2_ref[...] = h2.astype(BF16)
    logits = _nt_dot(wrt_ref[...], h2, precision=lax.Precision.HIGHEST)
    e = jnp.exp(logits - jnp.max(logits, axis=0, keepdims=True))
    aff_ref[...] = e / jnp.sum(e, axis=0, keepdims=True)


def _merge(x, o_a, o_b, proj, w_oa, w_ob, w_out, nf, w_rt, *, tm=512):
    T = x.shape[0]
    full = lambda i: (0, 0)
    return pl.pallas_call(
        _merge_kernel,
        grid=(T // tm,),
        in_specs=[
            pl.BlockSpec((tm, D_MODEL), lambda i: (i, 0)),
            pl.BlockSpec((tm, GLA_V), lambda i: (i, 0)),
            pl.BlockSpec((tm, ATT_Q), lambda i: (i, 0)),
            pl.BlockSpec((tm, D_MODEL), lambda i: (i, COL_ZA * LANE // D_MODEL)),
            pl.BlockSpec((tm, D_MODEL), lambda i: (i, COL_ZB * LANE // D_MODEL)),
            pl.BlockSpec((GLA_V, D_MODEL), full),
            pl.BlockSpec((ATT_Q, D_MODEL), full),
            pl.BlockSpec((D_MODEL, D_MODEL), full),
            pl.BlockSpec((1, D_MODEL), full),
            pl.BlockSpec((N_EXPERTS, D_MODEL), full),
        ],
        out_specs=[
            pl.BlockSpec((tm, D_MODEL), lambda i: (i, 0)),
            pl.BlockSpec((tm, D_MODEL), lambda i: (i, 0)),
            pl.BlockSpec((N_EXPERTS, tm), lambda i: (0, i)),
        ],
        out_shape=[
            jax.ShapeDtypeStruct((T, D_MODEL), F32),
            jax.ShapeDtypeStruct((T, D_MODEL), BF16),
            jax.ShapeDtypeStruct((N_EXPERTS, T), F32),
        ],
        compiler_params=_params("parallel"),
        name="merge",
    )(x, o_a, o_b, proj, proj, w_oa, w_ob, w_out, nf, w_rt)


def _lane_cumsum(x):
    n = x.shape[1]
    lane = lax.broadcasted_iota(I32, x.shape, 1)
    shift = 1
    while shift < n:
        x = x + jnp.where(lane >= shift, pltpu.roll(x, shift, 1), 0)
        shift *= 2
    return x


def _select_kernel(aff_ref, slot_ref, incl_ref, *, cap):
    bits = pltpu.bitcast(aff_ref[...], I32)

    def body(i, thr):
        cand = thr | jnp.left_shift(jnp.int32(1), 30 - i)
        cnt = jnp.sum((bits >= cand).astype(I32), axis=1, keepdims=True)
        return jnp.where(cnt >= cap, cand, thr)

    thr = lax.fori_loop(0, 31, body, jnp.zeros((bits.shape[0], 1), I32))
    above = bits > thr
    need = cap - jnp.sum(above.astype(I32), axis=1, keepdims=True)
    ties = bits == thr
    sel = above | (ties & (_lane_cumsum(ties.astype(I32)) <= need))
    incl = _lane_cumsum(sel.astype(I32))
    slot_ref[...] = jnp.where(sel, incl - 1, -1)
    incl_ref[...] = incl


def _select(aff_t, cap):
    E, T = aff_t.shape
    full = lambda: (0, 0)
    slot, incl = pl.pallas_call(
        functools.partial(_select_kernel, cap=cap),
        in_specs=[pl.BlockSpec((E, T), full)],
        out_specs=[pl.BlockSpec((E, T), full), pl.BlockSpec((E, T), full)],
        out_shape=[jax.ShapeDtypeStruct((E, T), I32), jax.ShapeDtypeStruct((E, T), I32)],
        compiler_params=pltpu.CompilerParams(vmem_limit_bytes=VMEM_LIMIT),
        name="select",
    )(aff_t)
    def before(step):
        return jnp.concatenate([jnp.zeros((E, 1), I32), incl[:, step - 1::step]], axis=1)

    return slot, before(GATHER_TC), before(ROUTE_TB)


def _gather_kernel(offs_ref, slot_ref, gate_ref, h_ref, xe_ref, gc_ref, acc_ref, g_ref, ptr_ref):
    e = pl.program_id(0)
    base = pl.program_id(1) * ROUTE_WIN

    first = lax.while_loop(lambda c: offs_ref[e, c + 1] <= base, lambda c: c + 1,
                           jnp.where(base == 0, 0, ptr_ref[0]))
    ptr_ref[0] = first
    stop = lax.while_loop(lambda c: offs_ref[e, c] < base + ROUTE_WIN, lambda c: c + 1, first)

    acc_ref[...] = jnp.zeros_like(acc_ref)
    g_ref[...] = jnp.zeros_like(g_ref)
    want = lax.broadcasted_iota(I32, (ROUTE_WIN, GATHER_TC), 0) + base

    def body(c, carry):
        hit = slot_ref[0, pl.ds(c, 1), :] == want
        rows = pl.ds(pl.multiple_of(c * GATHER_TC, GATHER_TC), GATHER_TC)
        acc_ref[...] += _dot(jnp.where(hit, 1.0, 0.0).astype(BF16), h_ref[rows, :])
        g_ref[...] += jnp.sum(jnp.where(hit, gate_ref[0, pl.ds(c, 1), :], 0.0), axis=1, keepdims=True)
        return carry

    lax.fori_loop(first, stop, body, 0)
    xe_ref[0] = acc_ref[...].astype(xe_ref.dtype)
    gc_ref[0] = g_ref[...]


def _gather(offs, slot, aff_t, h2, cap):
    E, T = slot.shape
    assert cap % ROUTE_WIN == 0 and T % GATHER_TC == 0
    n_chunks = T // GATHER_TC
    row_spec = pl.BlockSpec((1, n_chunks, GATHER_TC), lambda e, w, offs: (e, 0, 0))
    return pl.pallas_call(
        _gather_kernel,
        grid_spec=pltpu.PrefetchScalarGridSpec(
            num_scalar_prefetch=1,
            grid=(E, cap // ROUTE_WIN),
            in_specs=[row_spec, row_spec,
                      pl.BlockSpec((T, D_MODEL), lambda e, w, offs: (0, 0), pipeline_mode=pl.Buffered(1))],
            out_specs=[
                pl.BlockSpec((1, ROUTE_WIN, D_MODEL), lambda e, w, offs: (e, w, 0)),
                pl.BlockSpec((1, ROUTE_WIN, 1), lambda e, w, offs: (e, w, 0)),
            ],
            scratch_shapes=[pltpu.VMEM((ROUTE_WIN, D_MODEL), F32), pltpu.VMEM((ROUTE_WIN, 1), F32),
                            pltpu.SMEM((1,), I32)],
        ),
        out_shape=[
            jax.ShapeDtypeStruct((E, cap, D_MODEL), BF16),
            jax.ShapeDtypeStruct((E, cap, 1), F32),
        ],
        compiler_params=_params("arbitrary", "arbitrary"),
        name="gather",
    )(offs, slot.reshape(E, n_chunks, GATHER_TC), aff_t.reshape(E, n_chunks, GATHER_TC), h2)


def _scatter_window(offs_ref, e, t, cap, rows):
    return pl.multiple_of(jnp.minimum((offs_ref[e, t] // ROUTE_WIN) * ROUTE_WIN, cap - rows), ROUTE_WIN)


def _scatter_copy(offs_ref, y_hbm, ybuf, sem, e, t, buf, cap, rows):
    start = _scatter_window(offs_ref, e, t, cap, rows)
    return pltpu.make_async_copy(y_hbm.at[e, pl.ds(start, rows), :],
                                 ybuf.at[buf, pl.ds(e * rows, rows), :], sem.at[buf])


def _scatter_kernel(offs_ref, slot_ref, x1_ref, y_hbm, o_ref, ybuf, sem, *, cap, rows):
    t = pl.program_id(0)
    buf = t % 2
    copy = functools.partial(_scatter_copy, offs_ref, y_hbm, ybuf, sem, cap=cap, rows=rows)

    @pl.when(t == 0)
    def _():
        for e in range(N_EXPERTS):
            copy(e, t, buf).start()

    @pl.when(t + 1 < pl.num_programs(0))
    def _():
        for e in range(N_EXPERTS):
            copy(e, t + 1, 1 - buf).start()

    lane = lax.broadcasted_iota(I32, (ROUTE_TB, rows), 1)
    hits = []
    for e in range(N_EXPERTS):
        rel = slot_ref[:, e:e + 1] - _scatter_window(offs_ref, e, t, cap, rows)
        hits.append(jnp.where(rel == lane, 1.0, 0.0).astype(BF16))
    onehot = jnp.concatenate(hits, axis=1)

    for e in range(N_EXPERTS):
        copy(e, t, buf).wait()
    o_ref[...] = x1_ref[...] + _dot(onehot, ybuf[buf])


def _scatter(offs, slot, x1, ye):
    E, T = slot.shape
    cap = ye.shape[1]
    rows = min(ROUTE_NWIN * ROUTE_WIN, cap)
    return pl.pallas_call(
        functools.partial(_scatter_kernel, cap=cap, rows=rows),
        grid_spec=pltpu.PrefetchScalarGridSpec(
            num_scalar_prefetch=1,
            grid=(T // ROUTE_TB,),
            in_specs=[
                pl.BlockSpec((ROUTE_TB, E), lambda t, offs: (t, 0)),
                pl.BlockSpec((ROUTE_TB, D_MODEL), lambda t, offs: (t, 0)),
                pl.BlockSpec(memory_space=pl.ANY),
            ],
            out_specs=pl.BlockSpec((ROUTE_TB, D_MODEL), lambda t, offs: (t, 0)),
            scratch_shapes=[pltpu.VMEM((2, E * rows, D_MODEL), BF16), pltpu.SemaphoreType.DMA((2,))],
        ),
        out_shape=jax.ShapeDtypeStruct((T, D_MODEL), F32),
        compiler_params=_params("arbitrary"),
        name="scatter",
    )(offs, slot.T, x1, ye)


def _ffn_kernel(xe_ref, gate_ref, wg_ref, wu_ref, wd_ref, o_ref, acc_ref, *, rows):
    f = pl.program_id(1)
    wg = wg_ref[0].astype(BF16)
    wu = wu_ref[0].astype(BF16)
    wd = wd_ref[0].astype(BF16)
    cap = xe_ref.shape[1]

    def body(i, carry):
        sl = pl.ds(pl.multiple_of(i * rows, rows), rows)
        xe = xe_ref[0, sl, :]
        hg = _dot(xe, wg)
        hu = _dot(xe, wu)
        hid = (hg * jax.nn.sigmoid(hg) * hu).astype(BF16)
        y = _dot(hid, wd)

        @pl.when(f == 0)
        def _():
            acc_ref[sl, :] = y

        @pl.when(f != 0)
        def _():
            acc_ref[sl, :] += y
        return carry

    lax.fori_loop(0, cap // rows, body, 0)

    @pl.when(f == pl.num_programs(1) - 1)
    def _():
        o_ref[0] = (acc_ref[...] * gate_ref[0]).astype(o_ref.dtype)


def _ffn(xe, gate, w_g, w_u, w_d, *, tf=512, rows=256):
    E, cap, _ = xe.shape
    rows = min(rows, cap)
    return pl.pallas_call(
        functools.partial(_ffn_kernel, rows=rows),
        grid=(E, EXPERT_FF // tf),
        in_specs=[
            pl.BlockSpec((1, cap, D_MODEL), lambda e, f: (e, 0, 0)),
            pl.BlockSpec((1, cap, 1), lambda e, f: (e, 0, 0)),
            pl.BlockSpec((1, D_MODEL, tf), lambda e, f: (e, 0, f)),
            pl.BlockSpec((1, D_MODEL, tf), lambda e, f: (e, 0, f)),
            pl.BlockSpec((1, tf, D_MODEL), lambda e, f: (e, f, 0)),
        ],
        out_specs=pl.BlockSpec((1, cap, D_MODEL), lambda e, f: (e, 0, 0)),
        out_shape=jax.ShapeDtypeStruct((E, cap, D_MODEL), BF16),
        scratch_shapes=[pltpu.VMEM((cap, D_MODEL), F32)],
        compiler_params=_params("parallel", "arbitrary"),
        name="ffn",
    )(xe, gate, w_g, w_u, w_d)


def _prep_weights(w_in, w_gk_f, w_gk_b, q_norm, k_norm):
    pts = np.cumsum([GLA_QK, GLA_QK, GLA_V, GLA_V, GLA_RANK, GLA_RANK, ATT_Q, ATT_KV, ATT_KV, D_MODEL])
    gq, gk, gv, gg, r_f, r_b, aq, ak, av, z_a, z_b = jnp.split(w_in, pts, axis=-1)
    perm = np.concatenate([np.arange(0, ATT_HD, 2), np.arange(1, ATT_HD, 2)])
    perm_q = (np.arange(ATT_HEADS)[:, None] * ATT_HD + perm[None, :]).reshape(-1)
    perm_k = (np.arange(ATT_KV_HEADS)[:, None] * ATT_HD + perm[None, :]).reshape(-1)
    w_main = jnp.concatenate([gq, gk, gv, gg, z_a, z_b, aq[:, perm_q], ak[:, perm_k], av], axis=-1)
    w_r = jnp.pad(jnp.concatenate([r_f, r_b], axis=-1), ((0, 0), (0, LANE - 2 * GLA_RANK)))
    wf = jnp.pad(w_gk_f, ((0, LANE - GLA_RANK), (0, 0)))
    wb = jnp.pad(w_gk_b, ((GLA_RANK, LANE - 2 * GLA_RANK), (0, 0)))
    return (w_main.astype(BF16), w_r.astype(BF16), wf.astype(BF16), wb.astype(BF16),
            q_norm[perm][None, :], k_norm[perm][None, :])


def _layer(x3, norm_mix, w_main, w_r, wf, bf, wb, bb, gla_norm, qn, kn,
           w_oa, w_ob, w_out, norm_ffn, w_rt, w_e_gate, w_e_up, w_e_down):
    B, S, D = x3.shape
    T = B * S
    x = x3.reshape(T, D)
    proj, r = _inproj(x, norm_mix, w_r, w_main)
    proj3, r3 = proj.reshape(B, S, -1), r.reshape(B, S, LANE)
    o_f = _gla_dir(proj3, r3, wf, bf, True)
    o_a = _gla_dir(proj3, r3, wb, bb, False, o_f, gla_norm)
    cos, sin = _rope_tables(S)
    q, k, v = _qkprep(proj, cos, sin, qn, kn, S)
    o_b = _flash(q.reshape(B, S, ATT_Q), k.reshape(B, S, ATT_KV), v.reshape(B, S, 2 * ATT_KV))
    x1, h2, aff_t = _merge(x, o_a.reshape(T, GLA_V), o_b.reshape(T, ATT_Q), proj,
                           w_oa, w_ob, w_out, norm_ffn, w_rt)
    cap = CAPACITY_FACTOR * T // N_EXPERTS
    slot, offs_gather, offs_scatter = _select(aff_t, cap)
    xe, gate = _gather(offs_gather, slot, aff_t, h2, cap)
    ye = _ffn(xe, gate, w_e_gate, w_e_up, w_e_down)
    out = _scatter(offs_scatter, slot, x1, ye)
    return out.reshape(B, S, D)


def kernel(x_prompt, x_sample, norm_mix, w_in, w_gk_f, b_gk_f, w_gk_b, b_gk_b, gla_norm, q_norm, k_norm,
           w_o_gla, w_o_att, w_out, norm_ffn, w_router, w_e_gate, w_e_up, w_e_down):
    w_main, w_r, wf, wb, qn, kn = _prep_weights(w_in[0], w_gk_f[0], w_gk_b[0], q_norm[0], k_norm[0])
    args = (norm_mix[0][None, :], w_main, w_r, wf, b_gk_f[0][None, :], wb, b_gk_b[0][None, :],
            gla_norm[0][None, :], qn, kn,
            w_o_gla[0].astype(BF16), w_o_att[0].astype(BF16), w_out[0].astype(BF16),
            norm_ffn[0][None, :], w_router[0].T, w_e_gate[0], w_e_up[0], w_e_down[0])
    return (_layer(x_prompt, *args), _layer(x_sample, *args))
```

```python
import functools

import jax
import jax.numpy as jnp
import numpy as np
from jax import lax
from jax.experimental import pallas as pl
from jax.experimental.pallas import tpu as pltpu

D_MODEL = 1024
GRID_W = 64
GLA_HEADS = 4
GLA_DK = 128
GLA_DV = 256
GLA_RANK = 16
GLA_GATE_NORM = 16.0
GLA_CHUNK = 64
GLA_QK = GLA_HEADS * GLA_DK
GLA_V = GLA_HEADS * GLA_DV
ATT_HEADS = 8
ATT_KV_HEADS = 2
ATT_GROUP = ATT_HEADS // ATT_KV_HEADS
ATT_HD = 128
ATT_Q = ATT_HEADS * ATT_HD
ATT_KV = ATT_KV_HEADS * ATT_HD
ROPE_THETA = 10000.0
Q_BLOCK = 128
N_EXPERTS = 16
CAPACITY_FACTOR = 2
EXPERT_FF = 2048
EPS = 1e-6
LOG2_E = 1.4426950408889634

LANE = 128
VMEM_LIMIT = 56 * 1024 * 1024

BF16 = jnp.bfloat16
F32 = jnp.float32
I32 = jnp.int32

COL_GQ, COL_GK, COL_GV, COL_GG = 0, 4, 8, 16
COL_ZA, COL_ZB, COL_AQ, COL_AK, COL_AV = 24, 32, 40, 48, 50
N_MAIN_COLS = 52 * LANE

GLA_BLOCK = 256
GLA_SUB = GLA_BLOCK // GLA_CHUNK
ROUTE_TB = 256
ROUTE_WIN = 128
ROUTE_NWIN = ROUTE_TB // ROUTE_WIN + 1
MERGE_SUB = 1
GATHER_TC = 512
GATHER_WIN = 128


def _nt_dot(a, b, **kw):
    return lax.dot_general(a, b, (((1,), (1,)), ((), ())), preferred_element_type=F32, **kw)


def _tn_dot(a, b):
    return lax.dot_general(a, b, (((0,), (0,)), ((), ())), preferred_element_type=F32)


def _dot(a, b):
    return jnp.dot(a, b, preferred_element_type=F32)


def _params(*sem):
    return pltpu.CompilerParams(dimension_semantics=sem, vmem_limit_bytes=VMEM_LIMIT)


def _inproj_kernel(x_ref, g_ref, wr_ref, w_ref, o_ref, r_ref, hn_ref):
    @pl.when(pl.program_id(1) == 0)
    def _():
        x = x_ref[...]
        ms = jnp.mean(x * x, axis=-1, keepdims=True)
        hn = (x * lax.rsqrt(ms + EPS) * g_ref[...]).astype(BF16)
        hn_ref[...] = hn
        r_ref[...] = _dot(hn, wr_ref[...])

    o_ref[...] = _dot(hn_ref[...], w_ref[...])


def _inproj(x, g, w_r, w_main, *, tm=1024, n_col_tiles=4):
    T = x.shape[0]
    tn = N_MAIN_COLS // n_col_tiles
    return pl.pallas_call(
        _inproj_kernel,
        grid=(T // tm, n_col_tiles),
        in_specs=[
            pl.BlockSpec((tm, D_MODEL), lambda i, j: (i, 0)),
            pl.BlockSpec((1, D_MODEL), lambda i, j: (0, 0)),
            pl.BlockSpec((D_MODEL, LANE), lambda i, j: (0, 0)),
            pl.BlockSpec((D_MODEL, tn), lambda i, j: (0, j)),
        ],
        out_specs=[
            pl.BlockSpec((tm, tn), lambda i, j: (i, j)),
            pl.BlockSpec((tm, LANE), lambda i, j: (i, 0)),
        ],
        out_shape=[
            jax.ShapeDtypeStruct((T, N_MAIN_COLS), F32),
            jax.ShapeDtypeStruct((T, LANE), F32),
        ],
        scratch_shapes=[pltpu.VMEM((tm, D_MODEL), BF16)],
        compiler_params=_params("parallel", "arbitrary"),
        name="inproj",
    )(x, g, w_r, w_main)


def _log_sigmoid(x):
    return jnp.minimum(x, 0.0) - jnp.log1p(jnp.exp(-jnp.abs(x)))


def _rows(parts):
    return jnp.concatenate([jnp.broadcast_to(p, (GLA_CHUNK, p.shape[-1])) for p in parts], axis=0)


def _gla_masks(fwd):
    n = GLA_BLOCK
    row = lax.broadcasted_iota(I32, (n, n), 0)
    col = lax.broadcasted_iota(I32, (n, n), 1)
    same = (row // GLA_CHUNK) == (col // GLA_CHUNK)
    tri = same & ((col <= row) if fwd else (col >= row))
    dist = (row // GLA_CHUNK - col // GLA_CHUNK) * (1 if fwd else -1)
    return tri, dist


def _gla_prep(q, k, r, w, bias, tri, fwd):
    C = GLA_CHUNK
    lg = _log_sigmoid(_dot(r, w) + bias) * (1.0 / GLA_GATE_NORM)
    hi = lg.astype(BF16)
    rem = lg - hi.astype(F32)
    mid = rem.astype(BF16)
    lo = (rem - mid.astype(F32)).astype(BF16)
    tri_b = jnp.where(tri, 1.0, 0.0).astype(BF16)
    b = _dot(tri_b, hi) + _dot(tri_b, mid) + _dot(tri_b, lo)
    last = (lambda s: s * C + C - 1) if fwd else (lambda s: s * C)
    tot = [b[last(s):last(s) + 1, :] for s in range(GLA_SUB)]
    zero = jnp.zeros_like(tot[0])
    before = [zero, tot[0], tot[0] + tot[1], tot[0] + tot[1] + tot[2]]
    after = [tot[1] + tot[2] + tot[3], tot[2] + tot[3], tot[3], zero]
    lead, trail = (before, after) if fwd else (after, before)

    q_t = q * jnp.exp(b) * (GLA_DK ** -0.5)
    k_t = (k * jnp.exp(-b)).astype(BF16)
    k_hat = k * jnp.exp(_rows(tot) - b)
    q_in = (q_t * _rows([jnp.exp(x) for x in lead])).astype(BF16)
    k_out = (k_hat * _rows([jnp.exp(x) for x in trail])).astype(BF16)
    k_hat = k_hat.astype(BF16)

    e1, e2, e12 = jnp.exp(tot[1]), jnp.exp(tot[2]), jnp.exp(tot[1] + tot[2])
    if fwd:
        q1, q2, q3 = q_t[C:], q_t[2 * C:], q_t[3 * C:]
    else:
        q1, q2, q3 = q_t[:3 * C], q_t[:2 * C], q_t[:C]
    q2 = q2 * jnp.concatenate([jnp.broadcast_to(e1, (C, GLA_DK)), jnp.broadcast_to(e2, (C, GLA_DK))], axis=0)
    q3 = q3 * e12
    q_far = jnp.concatenate([q1, q2, q3], axis=0).astype(BF16)
    decay = jnp.exp(tot[0] + tot[1] + tot[2] + tot[3])
    return q_t.astype(BF16), k_t, q_far, k_hat, q_in, k_out, decay


def _gla_finish(ops, v, st, tri, dist, fwd):
    C = GLA_CHUNK
    q_t, k_t, q_far, k_hat, q_in, k_out, decay = ops
    y = _nt_dot(q_far, k_hat)
    y1, y2, y3 = y[:3 * C], y[3 * C:5 * C], y[5 * C:]
    z = lambda rows: jnp.zeros((rows, GLA_BLOCK), F32)
    if fwd:
        y1, y2, y3 = (jnp.concatenate([z(C), y1], 0), jnp.concatenate([z(2 * C), y2], 0),
                      jnp.concatenate([z(3 * C), y3], 0))
    else:
        y1, y2, y3 = (jnp.concatenate([y1, z(C)], 0), jnp.concatenate([y2, z(2 * C)], 0),
                      jnp.concatenate([y3, z(3 * C)], 0))
    x0 = _nt_dot(q_t, k_t)
    a = jnp.where(tri, x0, jnp.where(dist == 1, y1, jnp.where(dist == 2, y2, jnp.where(dist == 3, y3, 0.0))))
    o = _dot(a.astype(BF16), v) + _nt_dot(q_in, st.astype(BF16))
    st_new = st * decay + _tn_dot(v, k_out)
    return o, st_new


def _gla_dir_kernel(*refs, L, fwd):
    if fwd:
        q_ref, k_ref, v_ref, r_ref, w_ref, b_ref, o_ref, st_ref = refs
    else:
        q_ref, k_ref, v_ref, r_ref, w_ref, b_ref, g_ref, of_ref, gn_ref, o_ref, st_ref = refs
    nb = L // GLA_BLOCK

    @pl.when(pl.program_id(1) == 0)
    def _():
        st_ref[...] = jnp.zeros_like(st_ref)

    tri, dist = _gla_masks(fwd)

    def body(i, carry):
        blk = i if fwd else nb - 1 - i
        sl = pl.ds(pl.multiple_of(blk * GLA_BLOCK, GLA_BLOCK), GLA_BLOCK)
        r = r_ref[0, sl, :].astype(BF16)

        def prep(h):
            ks = slice(h * GLA_DK, (h + 1) * GLA_DK)
            return _gla_prep(q_ref[0, sl, ks], k_ref[0, sl, ks], r, w_ref[:, ks], b_ref[:, ks], tri, fwd)

        nxt = prep(0)
        for h in range(GLA_HEADS):
            vs = slice(h * GLA_DV, (h + 1) * GLA_DV)
            ops = nxt
            if h + 1 < GLA_HEADS:
                nxt = prep(h + 1)
            o, st = _gla_finish(ops, v_ref[0, sl, vs].astype(BF16), st_ref[h], tri, dist, fwd)
            st_ref[h] = st
            if fwd:
                o_ref[0, sl, vs] = o
            else:
                o = o + of_ref[0, sl, vs]
                ms = jnp.mean(o * o, axis=-1, keepdims=True)
                o = o * lax.rsqrt(ms + EPS) * gn_ref[...]
                g = g_ref[0, sl, vs]
                o_ref[0, sl, vs] = (o * (g * jax.nn.sigmoid(g))).astype(o_ref.dtype)
        return carry

    lax.fori_loop(0, nb, body, 0)


def _gla_dir(proj3, r3, w, bias, fwd, o_f=None, gn=None, *, L=512):
    B, S, _ = proj3.shape
    nl = S // L
    pos = (lambda n: n) if fwd else (lambda n: nl - 1 - n)
    full = lambda b, n: (0, 0)
    in_specs = [
        pl.BlockSpec((1, L, GLA_QK), lambda b, n: (b, pos(n), COL_GQ * LANE // GLA_QK)),
        pl.BlockSpec((1, L, GLA_QK), lambda b, n: (b, pos(n), COL_GK * LANE // GLA_QK)),
        pl.BlockSpec((1, L, GLA_V), lambda b, n: (b, pos(n), COL_GV * LANE // GLA_V)),
        pl.BlockSpec((1, L, LANE), lambda b, n: (b, pos(n), 0)),
        pl.BlockSpec((LANE, GLA_QK), full),
        pl.BlockSpec((1, GLA_QK), full),
    ]
    args = [proj3, proj3, proj3, r3, w, bias]
    if not fwd:
        in_specs += [
            pl.BlockSpec((1, L, GLA_V), lambda b, n: (b, pos(n), COL_GG * LANE // GLA_V)),
            pl.BlockSpec((1, L, GLA_V), lambda b, n: (b, pos(n), 0)),
            pl.BlockSpec((1, GLA_DV), full),
        ]
        args += [proj3, o_f, gn]
    return pl.pallas_call(
        functools.partial(_gla_dir_kernel, L=L, fwd=fwd),
        grid=(B, nl),
        in_specs=in_specs,
        out_specs=pl.BlockSpec((1, L, GLA_V), lambda b, n: (b, pos(n), 0)),
        out_shape=jax.ShapeDtypeStruct((B, S, GLA_V), F32 if fwd else BF16),
        scratch_shapes=[pltpu.VMEM((GLA_HEADS, GLA_DV, GLA_DK), F32)],
        compiler_params=_params("parallel", "arbitrary"),
        name="gla_fwd" if fwd else "gla_bwd",
    )(*args)


def _rope_tables(S):
    rows = S // GRID_W
    pos_r = np.repeat(np.arange(rows), GRID_W).astype(np.float32)
    pos_c = np.tile(np.arange(GRID_W), rows).astype(np.float32)
    half = ATT_HD // 2
    inv = jnp.asarray(ROPE_THETA, F32) ** (-jnp.arange(0, half, 2, dtype=F32) / half)
    ang = jnp.concatenate([pos_r[:, None] * inv, pos_c[:, None] * inv], axis=-1)
    cos, sin = jnp.cos(ang), jnp.sin(ang)
    return jnp.concatenate([cos, cos], axis=-1), jnp.concatenate([-sin, sin], axis=-1)


def _qkprep_kernel(aq_ref, akv_ref, cos_ref, sin_ref, qn_ref, kn_ref, q_out, k_out, v_out):
    cos = cos_ref[...]
    sin = sin_ref[...]

    def norm_rope(x, w, scale):
        ms = jnp.mean(x * x, axis=-1, keepdims=True)
        y = x * lax.rsqrt(ms + EPS) * w
        return (y * cos + pltpu.roll(y, ATT_HD // 2, 1) * sin) * scale

    for h in range(ATT_HEADS):
        sl = slice(h * ATT_HD, (h + 1) * ATT_HD)
        q_out[:, sl] = norm_rope(aq_ref[:, sl], qn_ref[...], ATT_HD ** -0.5 * LOG2_E).astype(BF16)
    for h in range(ATT_KV_HEADS):
        sl = slice(h * ATT_HD, (h + 1) * ATT_HD)
        k_out[:, sl] = norm_rope(akv_ref[:, sl], kn_ref[...], 1.0).astype(BF16)
    for h in range(ATT_KV_HEADS):
        v_out[:, 2 * h * ATT_HD:(2 * h + 1) * ATT_HD] = akv_ref[:, ATT_KV + h * ATT_HD:ATT_KV + (h + 1) * ATT_HD].astype(BF16)
        v_out[:, (2 * h + 1) * ATT_HD:(2 * h + 2) * ATT_HD] = jnp.ones((v_out.shape[0], ATT_HD), BF16)


def _qkprep(proj, cos, sin, qn, kn, S, *, tm=512):
    T = proj.shape[0]
    ns = S // tm
    return pl.pallas_call(
        _qkprep_kernel,
        grid=(T // tm,),
        in_specs=[
            pl.BlockSpec((tm, ATT_Q), lambda i: (i, COL_AQ * LANE // ATT_Q)),
            pl.BlockSpec((tm, 2 * ATT_KV), lambda i: (i, COL_AK * LANE // (2 * ATT_KV))),
            pl.BlockSpec((tm, ATT_HD), lambda i: (i % ns, 0)),
            pl.BlockSpec((tm, ATT_HD), lambda i: (i % ns, 0)),
            pl.BlockSpec((1, ATT_HD), lambda i: (0, 0)),
            pl.BlockSpec((1, ATT_HD), lambda i: (0, 0)),
        ],
        out_specs=[
            pl.BlockSpec((tm, ATT_Q), lambda i: (i, 0)),
            pl.BlockSpec((tm, ATT_KV), lambda i: (i, 0)),
            pl.BlockSpec((tm, 2 * ATT_KV), lambda i: (i, 0)),
        ],
        out_shape=[
            jax.ShapeDtypeStruct((T, ATT_Q), BF16),
            jax.ShapeDtypeStruct((T, ATT_KV), BF16),
            jax.ShapeDtypeStruct((T, 2 * ATT_KV), BF16),
        ],
        compiler_params=_params("parallel"),
        name="qkprep",
    )(proj, proj, cos, sin, qn, kn)


def _flash_kernel(q_ref, k_ref, v_ref, o_ref, m_ref, acc_ref, *, tq):
    ki = pl.program_id(3)
    tk = k_ref.shape[1]

    @pl.when(ki == 0)
    def _():
        m_ref[...] = jnp.full_like(m_ref, -jnp.inf)
        acc_ref[...] = jnp.zeros_like(acc_ref)

    k = k_ref[0]
    v = v_ref[0]

    def scores(g):
        return _nt_dot(q_ref[0, :, g * ATT_HD:(g + 1) * ATT_HD], k)

    def update(g, s):
        part = s[:, :LANE]
        for j in range(1, tk // LANE):
            part = jnp.maximum(part, s[:, j * LANE:(j + 1) * LANE])
        m = m_ref[g]
        m_new = jnp.maximum(m, jnp.max(part, axis=-1, keepdims=True))
        p = jnp.exp2(s - m_new).astype(BF16)
        acc_ref[g] = jnp.exp2(m - m_new) * acc_ref[g] + _dot(p, v)
        m_ref[g] = m_new

    s_next = scores(0)
    for g in range(ATT_GROUP):
        s = s_next
        if g + 1 < ATT_GROUP:
            s_next = scores(g + 1)
        update(g, s)

    @pl.when(ki == pl.num_programs(3) - 1)
    def _():
        for g in range(ATT_GROUP):
            acc = acc_ref[g]
            out = acc[:, :ATT_HD] / acc[:, ATT_HD:ATT_HD + 1]
            for u in range(tq // Q_BLOCK):
                rows = slice(u * Q_BLOCK, (u + 1) * Q_BLOCK)
                o_ref[0, g, :, u * ATT_HD:(u + 1) * ATT_HD] = out[rows, :].astype(o_ref.dtype)


def _flash(q3, k3, v3, *, tq=512, tk=2048):
    B, S, _ = q3.shape
    gw = ATT_GROUP * ATT_HD
    n_qt = S // tq
    tk = min(tk, S)
    return pl.pallas_call(
        functools.partial(_flash_kernel, tq=tq),
        grid=(B, ATT_KV_HEADS, S // tq, S // tk),
        in_specs=[
            pl.BlockSpec((1, tq, gw), lambda b, h, qi, ki: (b, qi, h)),
            pl.BlockSpec((1, tk, ATT_HD), lambda b, h, qi, ki: (b, ki, h)),
            pl.BlockSpec((1, tk, 2 * ATT_HD), lambda b, h, qi, ki: (b, ki, h)),
        ],
        out_specs=pl.BlockSpec((1, ATT_GROUP, Q_BLOCK, tq), lambda b, h, qi, ki: (b, 0, 0, h * n_qt + qi)),
        out_shape=jax.ShapeDtypeStruct((B, ATT_GROUP, Q_BLOCK, ATT_KV_HEADS * S), BF16),
        scratch_shapes=[
            pltpu.VMEM((ATT_GROUP, tq, 1), F32),
            pltpu.VMEM((ATT_GROUP, tq, 2 * ATT_HD), F32),
        ],
        compiler_params=_params("parallel", "parallel", "parallel", "arbitrary"),
        name="flash",
    )(q3, k3, v3)


def _merge_kernel(x_ref, oa_ref, ob_ref, za_ref, zb_ref, woa_ref, wob_ref, wout_ref, nf_ref, wrt_ref,
                  x1_ref, h2_ref, aff_ref):
    tm = x_ref.shape[0]
    sub = tm // MERGE_SUB

    def branches(i):
        rows = slice(i * sub, (i + 1) * sub)
        return _dot(oa_ref[rows, :], woa_ref[...]), _dot(ob_ref[rows, :], wob_ref[...])

    nxt = branches(0)
    for i in range(MERGE_SUB):
        rows = slice(i * sub, (i + 1) * sub)
        ya, yb = nxt
        if i + 1 < MERGE_SUB:
            nxt = branches(i + 1)
        merged = jax.nn.sigmoid(za_ref[rows, :]) * ya + jax.nn.sigmoid(zb_ref[rows, :]) * yb
        x1 = x_ref[rows, :] + _dot(merged.astype(BF16), wout_ref[...])
        x1_ref[rows, :] = x1
        ms = jnp.mean(x1 * x1, axis=-1, keepdims=True)
        h2 = x1 * lax.rsqrt(ms + EPS) * nf_ref[...]
        h2_ref[rows, :] = h2.astype(BF16)
        logits = _nt_dot(wrt_ref[...], h2, precision=lax.Precision.HIGHEST)
        e = jnp.exp(logits - jnp.max(logits, axis=0, keepdims=True))
        aff_ref[:, rows] = e / jnp.sum(e, axis=0, keepdims=True)


def _merge(x, o_a, o_b, proj, w_oa, w_ob, w_out, nf, w_rt, *, tm=512):
    T = x.shape[0]
    full = lambda i: (0, 0)
    return pl.pallas_call(
        _merge_kernel,
        grid=(T // tm,),
        in_specs=[
            pl.BlockSpec((tm, D_MODEL), lambda i: (i, 0)),
            pl.BlockSpec((tm, GLA_V), lambda i: (i, 0)),
            pl.BlockSpec((tm, ATT_Q), lambda i: (i, 0)),
            pl.BlockSpec((tm, D_MODEL), lambda i: (i, COL_ZA * LANE // D_MODEL)),
            pl.BlockSpec((tm, D_MODEL), lambda i: (i, COL_ZB * LANE // D_MODEL)),
            pl.BlockSpec((GLA_V, D_MODEL), full),
            pl.BlockSpec((ATT_Q, D_MODEL), full),
            pl.BlockSpec((D_MODEL, D_MODEL), full),
            pl.BlockSpec((1, D_MODEL), full),
            pl.BlockSpec((N_EXPERTS, D_MODEL), full),
        ],
        out_specs=[
            pl.BlockSpec((tm, D_MODEL), lambda i: (i, 0)),
            pl.BlockSpec((tm, D_MODEL), lambda i: (i, 0)),
            pl.BlockSpec((N_EXPERTS, tm), lambda i: (0, i)),
        ],
        out_shape=[
            jax.ShapeDtypeStruct((T, D_MODEL), F32),
            jax.ShapeDtypeStruct((T, D_MODEL), BF16),
            jax.ShapeDtypeStruct((N_EXPERTS, T), F32),
        ],
        compiler_params=_params("parallel"),
        name="merge",
    )(x, o_a, o_b, proj, proj, w_oa, w_ob, w_out, nf, w_rt)


def _lane_cumsum(x):
    n = x.shape[1]
    lane = lax.broadcasted_iota(I32, x.shape, 1)
    shift = 1
    while shift < n:
        x = x + jnp.where(lane >= shift, pltpu.roll(x, shift, 1), 0)
        shift *= 2
    return x


def _select_kernel(aff_ref, slot_ref, incl_ref, *, cap):
    bits = pltpu.bitcast(aff_ref[...], I32)

    def body(i, thr):
        cand = thr | jnp.left_shift(jnp.int32(1), 30 - i)
        cnt = jnp.sum((bits >= cand).astype(I32), axis=1, keepdims=True)
        return jnp.where(cnt >= cap, cand, thr)

    thr = lax.fori_loop(0, 31, body, jnp.zeros((bits.shape[0], 1), I32))
    above = bits > thr
    need = cap - jnp.sum(above.astype(I32), axis=1, keepdims=True)
    ties = bits == thr
    sel = above | (ties & (_lane_cumsum(ties.astype(I32)) <= need))
    incl = _lane_cumsum(sel.astype(I32))
    slot_ref[...] = jnp.where(sel, incl - 1, -1)
    incl_ref[...] = incl


def _select(aff_t, cap):
    E, T = aff_t.shape
    full = lambda: (0, 0)
    slot, incl = pl.pallas_call(
        functools.partial(_select_kernel, cap=cap),
        in_specs=[pl.BlockSpec((E, T), full)],
        out_specs=[pl.BlockSpec((E, T), full), pl.BlockSpec((E, T), full)],
        out_shape=[jax.ShapeDtypeStruct((E, T), I32), jax.ShapeDtypeStruct((E, T), I32)],
        compiler_params=pltpu.CompilerParams(vmem_limit_bytes=VMEM_LIMIT),
        name="select",
    )(aff_t)
    def before(step):
        return jnp.concatenate([jnp.zeros((E, 1), I32), incl[:, step - 1::step]], axis=1)

    return slot, before(GATHER_TC), before(ROUTE_TB)


def _gather_kernel(offs_ref, slot_ref, gate_ref, h_ref, xe_ref, gc_ref, acc_ref, g_ref, ptr_ref):
    e = pl.program_id(0)
    base = pl.program_id(1) * GATHER_WIN

    first = lax.while_loop(lambda c: offs_ref[e, c + 1] <= base, lambda c: c + 1,
                           jnp.where(base == 0, 0, ptr_ref[0]))
    ptr_ref[0] = first
    stop = lax.while_loop(lambda c: offs_ref[e, c] < base + GATHER_WIN, lambda c: c + 1, first)

    acc_ref[...] = jnp.zeros_like(acc_ref)
    g_ref[...] = jnp.zeros_like(g_ref)
    want = lax.broadcasted_iota(I32, (GATHER_WIN, GATHER_TC), 0) + base

    def body(c, carry):
        hit = slot_ref[0, pl.ds(c, 1), :] == want
        rows = pl.ds(pl.multiple_of(c * GATHER_TC, GATHER_TC), GATHER_TC)
        acc_ref[...] += _dot(jnp.where(hit, 1.0, 0.0).astype(BF16), h_ref[rows, :])
        g_ref[...] += jnp.sum(jnp.where(hit, gate_ref[0, pl.ds(c, 1), :], 0.0), axis=1, keepdims=True)
        return carry

    lax.fori_loop(first, stop, body, 0)
    xe_ref[0] = acc_ref[...].astype(xe_ref.dtype)
    gc_ref[0] = g_ref[...]


def _gather(offs, slot, aff_t, h2, cap):
    E, T = slot.shape
    assert cap % GATHER_WIN == 0 and T % GATHER_TC == 0
    n_chunks = T // GATHER_TC
    row_spec = pl.BlockSpec((1, n_chunks, GATHER_TC), lambda e, w, offs: (e, 0, 0))
    return pl.pallas_call(
        _gather_kernel,
        grid_spec=pltpu.PrefetchScalarGridSpec(
            num_scalar_prefetch=1,
            grid=(E, cap // GATHER_WIN),
            in_specs=[row_spec, row_spec,
                      pl.BlockSpec((T, D_MODEL), lambda e, w, offs: (0, 0), pipeline_mode=pl.Buffered(1))],
            out_specs=[
                pl.BlockSpec((1, GATHER_WIN, D_MODEL), lambda e, w, offs: (e, w, 0)),
                pl.BlockSpec((1, GATHER_WIN, 1), lambda e, w, offs: (e, w, 0)),
            ],
            scratch_shapes=[pltpu.VMEM((GATHER_WIN, D_MODEL), F32), pltpu.VMEM((GATHER_WIN, 1), F32),
                            pltpu.SMEM((1,), I32)],
        ),
        out_shape=[
            jax.ShapeDtypeStruct((E, cap, D_MODEL), BF16),
            jax.ShapeDtypeStruct((E, cap, 1), F32),
        ],
        compiler_params=_params("arbitrary", "arbitrary"),
        name="gather",
    )(offs, slot.reshape(E, n_chunks, GATHER_TC), aff_t.reshape(E, n_chunks, GATHER_TC), h2)


def _scatter_window(offs_ref, e, t, cap, rows):
    return pl.multiple_of(jnp.minimum((offs_ref[e, t] // ROUTE_WIN) * ROUTE_WIN, cap - rows), ROUTE_WIN)


def _scatter_copy(offs_ref, y_hbm, ybuf, sem, e, t, buf, cap, rows):
    start = _scatter_window(offs_ref, e, t, cap, rows)
    return pltpu.make_async_copy(y_hbm.at[e, pl.ds(start, rows), :],
                                 ybuf.at[buf, pl.ds(e * rows, rows), :], sem.at[buf])


def _scatter_needs_wide(offs_ref, t, narrow):
    wide = False
    for e in range(N_EXPERTS):
        off = offs_ref[e, t]
        wide = wide | ((off % ROUTE_WIN) + (offs_ref[e, t + 1] - off) > narrow)
    return wide


def _scatter_kernel(offs_ref, slot_ref, x1_ref, y_hbm, o_ref, ybuf, sem, *, cap, narrow, rows):
    t = pl.program_id(0)
    buf = t % 2

    def by_width(tt, fn):
        wide = _scatter_needs_wide(offs_ref, tt, narrow)
        pl.when(wide)(functools.partial(fn, rows))
        pl.when(jnp.logical_not(wide))(functools.partial(fn, narrow))

    def start(tt, b, width):
        for e in range(N_EXPERTS):
            _scatter_copy(offs_ref, y_hbm, ybuf, sem, e, tt, b, cap, width).start()

    @pl.when(t == 0)
    def _():
        by_width(t, functools.partial(start, t, buf))

    @pl.when(t + 1 < pl.num_programs(0))
    def _():
        by_width(t + 1, functools.partial(start, t + 1, 1 - buf))

    def finish(width):
        lane = lax.broadcasted_iota(I32, (ROUTE_TB, width), 1)
        hits = []
        for e in range(N_EXPERTS):
            rel = slot_ref[:, e:e + 1] - _scatter_window(offs_ref, e, t, cap, width)
            hits.append(jnp.where(rel == lane, 1.0, 0.0).astype(BF16))
        onehot = jnp.concatenate(hits, axis=1)
        for e in range(N_EXPERTS):
            _scatter_copy(offs_ref, y_hbm, ybuf, sem, e, t, buf, cap, width).wait()
        o_ref[...] = x1_ref[...] + _dot(onehot, ybuf[buf, pl.ds(0, N_EXPERTS * width), :])

    by_width(t, finish)


def _scatter(offs, slot, x1, ye):
    E, T = slot.shape
    cap = ye.shape[1]
    rows = min(ROUTE_NWIN * ROUTE_WIN, cap)
    narrow = min(ROUTE_TB, cap)
    return pl.pallas_call(
        functools.partial(_scatter_kernel, cap=cap, narrow=narrow, rows=rows),
        grid_spec=pltpu.PrefetchScalarGridSpec(
            num_scalar_prefetch=1,
            grid=(T // ROUTE_TB,),
            in_specs=[
                pl.BlockSpec((ROUTE_TB, E), lambda t, offs: (t, 0)),
                pl.BlockSpec((ROUTE_TB, D_MODEL), lambda t, offs: (t, 0)),
                pl.BlockSpec(memory_space=pl.ANY),
            ],
            out_specs=pl.BlockSpec((ROUTE_TB, D_MODEL), lambda t, offs: (t, 0)),
            scratch_shapes=[pltpu.VMEM((2, E * rows, D_MODEL), BF16), pltpu.SemaphoreType.DMA((2,))],
        ),
        out_shape=jax.ShapeDtypeStruct((T, D_MODEL), F32),
        compiler_params=_params("arbitrary"),
        name="scatter",
    )(offs, slot.T, x1, ye)


def _ffn_kernel(xe_ref, gate_ref, wg_ref, wu_ref, wd_ref, o_ref, acc_ref, *, rows):
    f = pl.program_id(1)
    cap = xe_ref.shape[1]

    @pl.when(f == 0)
    def _():
        acc_ref[...] = jnp.zeros_like(acc_ref)

    wg = wg_ref[0].astype(BF16)
    wu = wu_ref[0].astype(BF16)

    def up(i):
        xe = xe_ref[0, i * rows:(i + 1) * rows, :]
        return _dot(xe, wg), _dot(xe, wu)

    nxt = up(0)
    wd = wd_ref[0].astype(BF16)
    for i in range(cap // rows):
        hg, hu = nxt
        if i + 1 < cap // rows:
            nxt = up(i + 1)
        hid = (hg * jax.nn.sigmoid(hg) * hu).astype(BF16)
        acc_ref[i * rows:(i + 1) * rows, :] += _dot(hid, wd)

    @pl.when(f == pl.num_programs(1) - 1)
    def _():
        o_ref[0] = (acc_ref[...] * gate_ref[0]).astype(o_ref.dtype)


def _ffn(xe, gate, w_g, w_u, w_d, *, tf=512, rows=1024):
    E, cap, _ = xe.shape
    rows = min(rows, cap)
    return pl.pallas_call(
        functools.partial(_ffn_kernel, rows=rows),
        grid=(E, EXPERT_FF // tf),
        in_specs=[
            pl.BlockSpec((1, cap, D_MODEL), lambda e, f: (e, 0, 0)),
            pl.BlockSpec((1, cap, 1), lambda e, f: (e, 0, 0)),
            pl.BlockSpec((1, D_MODEL, tf), lambda e, f: (e, 0, f)),
            pl.BlockSpec((1, D_MODEL, tf), lambda e, f: (e, 0, f)),
            pl.BlockSpec((1, tf, D_MODEL), lambda e, f: (e, f, 0)),
        ],
        out_specs=pl.BlockSpec((1, cap, D_MODEL), lambda e, f: (e, 0, 0)),
        out_shape=jax.ShapeDtypeStruct((E, cap, D_MODEL), BF16),
        scratch_shapes=[pltpu.VMEM((cap, D_MODEL), F32)],
        compiler_params=_params("parallel", "arbitrary"),
        name="ffn",
    )(xe, gate, w_g, w_u, w_d)


def _prep_weights(w_in, w_gk_f, w_gk_b, q_norm, k_norm):
    pts = np.cumsum([GLA_QK, GLA_QK, GLA_V, GLA_V, GLA_RANK, GLA_RANK, ATT_Q, ATT_KV, ATT_KV, D_MODEL])
    gq, gk, gv, gg, r_f, r_b, aq, ak, av, z_a, z_b = jnp.split(w_in, pts, axis=-1)
    perm = np.concatenate([np.arange(0, ATT_HD, 2), np.arange(1, ATT_HD, 2)])
    perm_q = (np.arange(ATT_HEADS)[:, None] * ATT_HD + perm[None, :]).reshape(-1)
    perm_k = (np.arange(ATT_KV_HEADS)[:, None] * ATT_HD + perm[None, :]).reshape(-1)
    w_main = jnp.concatenate([gq, gk, gv, gg, z_a, z_b, aq[:, perm_q], ak[:, perm_k], av], axis=-1)
    w_r = jnp.pad(jnp.concatenate([r_f, r_b], axis=-1), ((0, 0), (0, LANE - 2 * GLA_RANK)))
    wf = jnp.pad(w_gk_f, ((0, LANE - GLA_RANK), (0, 0)))
    wb = jnp.pad(w_gk_b, ((GLA_RANK, LANE - 2 * GLA_RANK), (0, 0)))
    return (w_main.astype(BF16), w_r.astype(BF16), wf.astype(BF16), wb.astype(BF16),
            q_norm[perm][None, :], k_norm[perm][None, :])


def _layer(x3, norm_mix, w_main, w_r, wf, bf, wb, bb, gla_norm, qn, kn,
           w_oa, w_ob, w_out, norm_ffn, w_rt, w_e_gate, w_e_up, w_e_down):
    B, S, D = x3.shape
    T = B * S
    x = x3.reshape(T, D)
    proj, r = _inproj(x, norm_mix, w_r, w_main)
    proj3, r3 = proj.reshape(B, S, -1), r.reshape(B, S, LANE)
    o_f = _gla_dir(proj3, r3, wf, bf, True)
    o_a = _gla_dir(proj3, r3, wb, bb, False, o_f, gla_norm)
    cos, sin = _rope_tables(S)
    q, k, v = _qkprep(proj, cos, sin, qn, kn, S)
    o_b = _flash(q.reshape(B, S, ATT_Q), k.reshape(B, S, ATT_KV), v.reshape(B, S, 2 * ATT_KV))
    x1, h2, aff_t = _merge(x, o_a.reshape(T, GLA_V), o_b.reshape(T, ATT_Q), proj,
                           w_oa, w_ob, w_out, norm_ffn, w_rt)
    cap = CAPACITY_FACTOR * T // N_EXPERTS
    slot, offs_gather, offs_scatter = _select(aff_t, cap)
    xe, gate = _gather(offs_gather, slot, aff_t, h2, cap)
    ye = _ffn(xe, gate, w_e_gate, w_e_up, w_e_down)
    out = _scatter(offs_scatter, slot, x1, ye)
    return out.reshape(B, S, D)


def kernel(x_prompt, x_sample, norm_mix, w_in, w_gk_f, b_gk_f, w_gk_b, b_gk_b, gla_norm, q_norm, k_norm,
           w_o_gla, w_o_att, w_out, norm_ffn, w_router, w_e_gate, w_e_up, w_e_down):
    w_main, w_r, wf, wb, qn, kn = _prep_weights(w_in[0], w_gk_f[0], w_gk_b[0], q_norm[0], k_norm[0])
    args = (norm_mix[0][None, :], w_main, w_r, wf, b_gk_f[0][None, :], wb, b_gk_b[0][None, :],
            gla_norm[0][None, :], qn, kn,
            w_o_gla[0].astype(BF16), w_o_att[0].astype(BF16), w_out[0].astype(BF16),
            norm_ffn[0][None, :], w_router[0].T, w_e_gate[0], w_e_up[0], w_e_down[0])
    return (_layer(x_prompt, *args), _layer(x_sample, *args))
```

```python
import functools

import jax
import jax.numpy as jnp
import numpy as np
from jax import lax
from jax.experimental import pallas as pl
from jax.experimental.pallas import tpu as pltpu

D_MODEL = 1024
GRID_W = 64
GLA_HEADS = 4
GLA_DK = 128
GLA_DV = 256
GLA_RANK = 16
GLA_GATE_NORM = 16.0
GLA_CHUNK = 64
GLA_QK = GLA_HEADS * GLA_DK
GLA_V = GLA_HEADS * GLA_DV
ATT_HEADS = 8
ATT_KV_HEADS = 2
ATT_GROUP = ATT_HEADS // ATT_KV_HEADS
ATT_HD = 128
ATT_Q = ATT_HEADS * ATT_HD
ATT_KV = ATT_KV_HEADS * ATT_HD
ROPE_THETA = 10000.0
Q_BLOCK = 128
N_EXPERTS = 16
CAPACITY_FACTOR = 2
EXPERT_FF = 2048
EPS = 1e-6
LOG2_E = 1.4426950408889634

LANE = 128
VMEM_LIMIT = 56 * 1024 * 1024

BF16 = jnp.bfloat16
F32 = jnp.float32
I32 = jnp.int32

COL_GQ, COL_GK, COL_GV, COL_GG = 0, 4, 8, 16
COL_ZA, COL_ZB, COL_AQ, COL_AK, COL_AV = 24, 32, 40, 48, 50
N_MAIN_COLS = 52 * LANE

GLA_BLOCK = 256
GLA_SUB = GLA_BLOCK // GLA_CHUNK
ROUTE_TB = 256
ROUTE_WIN = 128
ROUTE_NWIN = ROUTE_TB // ROUTE_WIN + 1
SCATTER_LEVELS = ((ROUTE_WIN, ROUTE_WIN // 2), (ROUTE_TB, ROUTE_WIN), (ROUTE_NWIN * ROUTE_WIN, ROUTE_WIN))
FLASH_TK = 4096
FLASH_SCORES = 512 * 4096
MIN_NORMAL_F32_BITS = 0x00800000
SELECT_REFINE_STEPS = 36
GATHER_TC = 512
GATHER_WIN = 128
GATHER_SPAN = 4


def _nt_dot(a, b, **kw):
    return lax.dot_general(a, b, (((1,), (1,)), ((), ())), preferred_element_type=F32, **kw)


def _tn_dot(a, b):
    return lax.dot_general(a, b, (((0,), (0,)), ((), ())), preferred_element_type=F32)


def _dot(a, b):
    return jnp.dot(a, b, preferred_element_type=F32)


def _params(*sem):
    return pltpu.CompilerParams(dimension_semantics=sem, vmem_limit_bytes=VMEM_LIMIT)


def _inproj_kernel(x_ref, g_ref, wr_ref, w_ref, o_ref, r_ref, hn_ref):
    @pl.when(pl.program_id(1) == 0)
    def _():
        x = x_ref[...]
        ms = jnp.mean(x * x, axis=-1, keepdims=True)
        hn = (x * lax.rsqrt(ms + EPS) * g_ref[...]).astype(BF16)
        hn_ref[...] = hn
        r_ref[...] = _dot(hn, wr_ref[...])

    o_ref[...] = _dot(hn_ref[...], w_ref[...])


def _inproj(x, g, w_r, w_main, *, tm=1024, n_col_tiles=4):
    T = x.shape[0]
    tn = N_MAIN_COLS // n_col_tiles
    return pl.pallas_call(
        _inproj_kernel,
        grid=(T // tm, n_col_tiles),
        in_specs=[
            pl.BlockSpec((tm, D_MODEL), lambda i, j: (i, 0)),
            pl.BlockSpec((1, D_MODEL), lambda i, j: (0, 0)),
            pl.BlockSpec((D_MODEL, LANE), lambda i, j: (0, 0)),
            pl.BlockSpec((D_MODEL, tn), lambda i, j: (0, j)),
        ],
        out_specs=[
            pl.BlockSpec((tm, tn), lambda i, j: (i, j)),
            pl.BlockSpec((tm, LANE), lambda i, j: (i, 0)),
        ],
        out_shape=[
            jax.ShapeDtypeStruct((T, N_MAIN_COLS), F32),
            jax.ShapeDtypeStruct((T, LANE), F32),
        ],
        scratch_shapes=[pltpu.VMEM((tm, D_MODEL), BF16)],
        compiler_params=_params("parallel", "arbitrary"),
        name="inproj",
    )(x, g, w_r, w_main)


def _log_sigmoid(x):
    return jnp.minimum(x, 0.0) - jnp.log(1.0 + jnp.exp(-jnp.abs(x)))


def _rows(parts):
    return jnp.concatenate([jnp.broadcast_to(p, (GLA_CHUNK, p.shape[-1])) for p in parts], axis=0)


def _gla_masks(fwd):
    n = GLA_BLOCK
    row = lax.broadcasted_iota(I32, (n, n), 0)
    col = lax.broadcasted_iota(I32, (n, n), 1)
    same = (row // GLA_CHUNK) == (col // GLA_CHUNK)
    tri = same & ((col <= row) if fwd else (col >= row))
    dist = (row // GLA_CHUNK - col // GLA_CHUNK) * (1 if fwd else -1)
    return tri, dist


def _gla_prep(q, k, r, w, bias, tri, fwd):
    C = GLA_CHUNK
    lg = _log_sigmoid(_dot(r, w) + bias) * (1.0 / GLA_GATE_NORM)
    hi = lg.astype(BF16)
    rem = lg - hi.astype(F32)
    mid = rem.astype(BF16)
    lo = (rem - mid.astype(F32)).astype(BF16)
    tri_b = jnp.where(tri, 1.0, 0.0).astype(BF16)
    b = _dot(tri_b, hi) + _dot(tri_b, mid) + _dot(tri_b, lo)
    last = (lambda s: s * C + C - 1) if fwd else (lambda s: s * C)
    tot = [b[last(s):last(s) + 1, :] for s in range(GLA_SUB)]
    zero = jnp.zeros_like(tot[0])
    before = [zero, tot[0], tot[0] + tot[1], tot[0] + tot[1] + tot[2]]
    after = [tot[1] + tot[2] + tot[3], tot[2] + tot[3], tot[3], zero]
    lead, trail = (before, after) if fwd else (after, before)

    q_t = q * jnp.exp(b) * (GLA_DK ** -0.5)
    k_t = (k * jnp.exp(-b)).astype(BF16)
    k_hat = k * jnp.exp(_rows(tot) - b)
    q_in = (q_t * _rows([jnp.exp(x) for x in lead])).astype(BF16)
    k_out = (k_hat * _rows([jnp.exp(x) for x in trail])).astype(BF16)
    k_hat = k_hat.astype(BF16)

    e1, e2, e12 = jnp.exp(tot[1]), jnp.exp(tot[2]), jnp.exp(tot[1] + tot[2])
    if fwd:
        q1, q2, q3 = q_t[C:], q_t[2 * C:], q_t[3 * C:]
    else:
        q1, q2, q3 = q_t[:3 * C], q_t[:2 * C], q_t[:C]
    q2 = q2 * jnp.concatenate([jnp.broadcast_to(e1, (C, GLA_DK)), jnp.broadcast_to(e2, (C, GLA_DK))], axis=0)
    q3 = q3 * e12
    q_far = jnp.concatenate([q1, q2, q3], axis=0).astype(BF16)
    decay = jnp.exp(tot[0] + tot[1] + tot[2] + tot[3])
    return q_t.astype(BF16), k_t, q_far, k_hat, q_in, k_out, decay


def _gla_finish(ops, v, st, tri, dist, fwd):
    C = GLA_CHUNK
    q_t, k_t, q_far, k_hat, q_in, k_out, decay = ops
    y = _nt_dot(q_far, k_hat)
    y1, y2, y3 = y[:3 * C], y[3 * C:5 * C], y[5 * C:]
    z = lambda rows: jnp.zeros((rows, GLA_BLOCK), F32)
    if fwd:
        y1, y2, y3 = (jnp.concatenate([z(C), y1], 0), jnp.concatenate([z(2 * C), y2], 0),
                      jnp.concatenate([z(3 * C), y3], 0))
    else:
        y1, y2, y3 = (jnp.concatenate([y1, z(C)], 0), jnp.concatenate([y2, z(2 * C)], 0),
                      jnp.concatenate([y3, z(3 * C)], 0))
    x0 = _nt_dot(q_t, k_t)
    a = jnp.where(tri, x0, jnp.where(dist == 1, y1, jnp.where(dist == 2, y2, jnp.where(dist == 3, y3, 0.0))))
    o = _dot(a.astype(BF16), v) + _nt_dot(q_in, st.astype(BF16))
    st_new = st * decay + _tn_dot(v, k_out)
    return o, st_new


def _gla_dir_kernel(*refs, L, fwd):
    if fwd:
        q_ref, k_ref, v_ref, r_ref, w_ref, b_ref, o_ref, st_ref = refs
    else:
        q_ref, k_ref, v_ref, r_ref, w_ref, b_ref, g_ref, of_ref, gn_ref, o_ref, st_ref = refs
    nb = L // GLA_BLOCK

    @pl.when(pl.program_id(1) == 0)
    def _():
        st_ref[...] = jnp.zeros_like(st_ref)

    tri, dist = _gla_masks(fwd)

    def body(i, carry):
        blk = i if fwd else nb - 1 - i
        sl = pl.ds(pl.multiple_of(blk * GLA_BLOCK, GLA_BLOCK), GLA_BLOCK)
        r = r_ref[0, sl, :].astype(BF16)

        def prep(h):
            ks = slice(h * GLA_DK, (h + 1) * GLA_DK)
            return _gla_prep(q_ref[0, sl, ks], k_ref[0, sl, ks], r, w_ref[:, ks], b_ref[:, ks], tri, fwd)

        nxt = prep(0)
        for h in range(GLA_HEADS):
            vs = slice(h * GLA_DV, (h + 1) * GLA_DV)
            ops = nxt
            if h + 1 < GLA_HEADS:
                nxt = prep(h + 1)
            o, st = _gla_finish(ops, v_ref[0, sl, vs].astype(BF16), st_ref[h], tri, dist, fwd)
            st_ref[h] = st
            if fwd:
                o_ref[0, sl, vs] = o
            else:
                o = o + of_ref[0, sl, vs]
                ms = jnp.mean(o * o, axis=-1, keepdims=True)
                o = o * lax.rsqrt(ms + EPS) * gn_ref[...]
                g = g_ref[0, sl, vs]
                o_ref[0, sl, vs] = (o * (g * jax.nn.sigmoid(g))).astype(o_ref.dtype)
        return carry

    lax.fori_loop(0, nb, body, 0)


def _gla_dir(proj3, r3, w, bias, fwd, o_f=None, gn=None, *, L=512):
    B, S, _ = proj3.shape
    nl = S // L
    pos = (lambda n: n) if fwd else (lambda n: nl - 1 - n)
    full = lambda b, n: (0, 0)
    in_specs = [
        pl.BlockSpec((1, L, GLA_QK), lambda b, n: (b, pos(n), COL_GQ * LANE // GLA_QK)),
        pl.BlockSpec((1, L, GLA_QK), lambda b, n: (b, pos(n), COL_GK * LANE // GLA_QK)),
        pl.BlockSpec((1, L, GLA_V), lambda b, n: (b, pos(n), COL_GV * LANE // GLA_V)),
        pl.BlockSpec((1, L, LANE), lambda b, n: (b, pos(n), 0)),
        pl.BlockSpec((LANE, GLA_QK), full),
        pl.BlockSpec((1, GLA_QK), full),
    ]
    args = [proj3, proj3, proj3, r3, w, bias]
    if not fwd:
        in_specs += [
            pl.BlockSpec((1, L, GLA_V), lambda b, n: (b, pos(n), COL_GG * LANE // GLA_V)),
            pl.BlockSpec((1, L, GLA_V), lambda b, n: (b, pos(n), 0)),
            pl.BlockSpec((1, GLA_DV), full),
        ]
        args += [proj3, o_f, gn]
    return pl.pallas_call(
        functools.partial(_gla_dir_kernel, L=L, fwd=fwd),
        grid=(B, nl),
        in_specs=in_specs,
        out_specs=pl.BlockSpec((1, L, GLA_V), lambda b, n: (b, pos(n), 0)),
        out_shape=jax.ShapeDtypeStruct((B, S, GLA_V), F32 if fwd else BF16),
        scratch_shapes=[pltpu.VMEM((GLA_HEADS, GLA_DV, GLA_DK), F32)],
        compiler_params=_params("parallel", "arbitrary"),
        name="gla_fwd" if fwd else "gla_bwd",
    )(*args)


def _rope_tables(S):
    rows = S // GRID_W
    pos_r = np.repeat(np.arange(rows), GRID_W).astype(np.float32)
    pos_c = np.tile(np.arange(GRID_W), rows).astype(np.float32)
    half = ATT_HD // 2
    inv = jnp.asarray(ROPE_THETA, F32) ** (-jnp.arange(0, half, 2, dtype=F32) / half)
    ang = jnp.concatenate([pos_r[:, None] * inv, pos_c[:, None] * inv], axis=-1)
    cos, sin = jnp.cos(ang), jnp.sin(ang)
    return jnp.concatenate([cos, cos], axis=-1), jnp.concatenate([-sin, sin], axis=-1)


def _qkprep_kernel(aq_ref, akv_ref, cos_ref, sin_ref, qn_ref, kn_ref, q_out, k_out, v_out):
    cos = cos_ref[...]
    sin = sin_ref[...]

    def norm_rope(x, w, scale):
        ms = jnp.mean(x * x, axis=-1, keepdims=True)
        y = x * lax.rsqrt(ms + EPS) * w
        return (y * cos + pltpu.roll(y, ATT_HD // 2, 1) * sin) * scale

    for h in range(ATT_HEADS):
        sl = slice(h * ATT_HD, (h + 1) * ATT_HD)
        q_out[:, sl] = norm_rope(aq_ref[:, sl], qn_ref[...], ATT_HD ** -0.5 * LOG2_E).astype(BF16)
    for h in range(ATT_KV_HEADS):
        sl = slice(h * ATT_HD, (h + 1) * ATT_HD)
        k_out[:, sl] = norm_rope(akv_ref[:, sl], kn_ref[...], 1.0).astype(BF16)
    for h in range(ATT_KV_HEADS):
        v_out[:, 2 * h * ATT_HD:(2 * h + 1) * ATT_HD] = akv_ref[:, ATT_KV + h * ATT_HD:ATT_KV + (h + 1) * ATT_HD].astype(BF16)
        v_out[:, (2 * h + 1) * ATT_HD:(2 * h + 2) * ATT_HD] = jnp.ones((v_out.shape[0], ATT_HD), BF16)


def _qkprep(proj, cos, sin, qn, kn, S, *, tm=512):
    T = proj.shape[0]
    ns = S // tm
    return pl.pallas_call(
        _qkprep_kernel,
        grid=(T // tm,),
        in_specs=[
            pl.BlockSpec((tm, ATT_Q), lambda i: (i, COL_AQ * LANE // ATT_Q)),
            pl.BlockSpec((tm, 2 * ATT_KV), lambda i: (i, COL_AK * LANE // (2 * ATT_KV))),
            pl.BlockSpec((tm, ATT_HD), lambda i: (i % ns, 0)),
            pl.BlockSpec((tm, ATT_HD), lambda i: (i % ns, 0)),
            pl.BlockSpec((1, ATT_HD), lambda i: (0, 0)),
            pl.BlockSpec((1, ATT_HD), lambda i: (0, 0)),
        ],
        out_specs=[
            pl.BlockSpec((tm, ATT_Q), lambda i: (i, 0)),
            pl.BlockSpec((tm, ATT_KV), lambda i: (i, 0)),
            pl.BlockSpec((tm, 2 * ATT_KV), lambda i: (i, 0)),
        ],
        out_shape=[
            jax.ShapeDtypeStruct((T, ATT_Q), BF16),
            jax.ShapeDtypeStruct((T, ATT_KV), BF16),
            jax.ShapeDtypeStruct((T, 2 * ATT_KV), BF16),
        ],
        compiler_params=_params("parallel"),
        name="qkprep",
    )(proj, proj, cos, sin, qn, kn)


def _flash_kernel(q_ref, k_ref, v_ref, o_ref, m_ref, acc_ref, *, tq):
    ki = pl.program_id(3)
    tk = k_ref.shape[1]

    @pl.when(ki == 0)
    def _():
        m_ref[...] = jnp.full_like(m_ref, -jnp.inf)
        acc_ref[...] = jnp.zeros_like(acc_ref)

    k = k_ref[0]
    v = v_ref[0]

    def scores(g):
        return _nt_dot(q_ref[0, :, g * ATT_HD:(g + 1) * ATT_HD], k)

    def update(g, s):
        part = s[:, :LANE]
        for j in range(1, tk // LANE):
            part = jnp.maximum(part, s[:, j * LANE:(j + 1) * LANE])
        m = m_ref[g]
        m_new = jnp.maximum(m, jnp.max(part, axis=-1, keepdims=True))
        p = jnp.exp2(s - m_new).astype(BF16)
        acc_ref[g] = jnp.exp2(m - m_new) * acc_ref[g] + _dot(p, v)
        m_ref[g] = m_new

    s_next = scores(0)
    for g in range(ATT_GROUP):
        s = s_next
        if g + 1 < ATT_GROUP:
            s_next = scores(g + 1)
        update(g, s)

    @pl.when(ki == pl.num_programs(3) - 1)
    def _():
        for g in range(ATT_GROUP):
            acc = acc_ref[g]
            out = acc[:, :ATT_HD] / acc[:, ATT_HD:ATT_HD + 1]
            for u in range(tq // Q_BLOCK):
                rows = slice(u * Q_BLOCK, (u + 1) * Q_BLOCK)
                o_ref[0, g, :, u * ATT_HD:(u + 1) * ATT_HD] = out[rows, :].astype(o_ref.dtype)


def _flash(q3, k3, v3):
    B, S, _ = q3.shape
    gw = ATT_GROUP * ATT_HD
    tk = min(FLASH_TK, S)
    tq = min(FLASH_SCORES // tk, S)
    n_qt = S // tq
    return pl.pallas_call(
        functools.partial(_flash_kernel, tq=tq),
        grid=(B, ATT_KV_HEADS, S // tq, S // tk),
        in_specs=[
            pl.BlockSpec((1, tq, gw), lambda b, h, qi, ki: (b, qi, h)),
            pl.BlockSpec((1, tk, ATT_HD), lambda b, h, qi, ki: (b, ki, h)),
            pl.BlockSpec((1, tk, 2 * ATT_HD), lambda b, h, qi, ki: (b, ki, h)),
        ],
        out_specs=pl.BlockSpec((1, ATT_GROUP, Q_BLOCK, tq), lambda b, h, qi, ki: (b, 0, 0, h * n_qt + qi)),
        out_shape=jax.ShapeDtypeStruct((B, ATT_GROUP, Q_BLOCK, ATT_KV_HEADS * S), BF16),
        scratch_shapes=[
            pltpu.VMEM((ATT_GROUP, tq, 1), F32),
            pltpu.VMEM((ATT_GROUP, tq, 2 * ATT_HD), F32),
        ],
        compiler_params=_params("parallel", "parallel", "parallel", "arbitrary"),
        name="flash",
    )(q3, k3, v3)


def _merge_kernel(x_ref, oa_ref, ob_ref, za_ref, zb_ref, woa_ref, wob_ref, wout_ref, nf_ref, wr_ref,
                  x1_ref, h2_ref, aff_ref):
    ya = _dot(oa_ref[...], woa_ref[...])
    yb = _dot(ob_ref[...], wob_ref[...])
    merged = jax.nn.sigmoid(za_ref[...]) * ya + jax.nn.sigmoid(zb_ref[...]) * yb
    x1 = x_ref[...] + _dot(merged.astype(BF16), wout_ref[...])
    x1_ref[...] = x1
    ms = jnp.mean(x1 * x1, axis=-1, keepdims=True)
    h2 = x1 * lax.rsqrt(ms + EPS) * nf_ref[...]
    h2_ref[...] = h2.astype(BF16)
    logits = _nt_dot(wr_ref[...], h2, precision=lax.Precision.HIGHEST)
    e = jnp.exp(logits - jnp.max(logits, axis=0, keepdims=True))
    aff_ref[...] = e / jnp.sum(e, axis=0, keepdims=True)


def _merge(x, o_a, o_b, proj, w_oa, w_ob, w_out, nf, w_rt, *, tm=512):
    T = x.shape[0]
    full = lambda i: (0, 0)
    return pl.pallas_call(
        _merge_kernel,
        grid=(T // tm,),
        in_specs=[
            pl.BlockSpec((tm, D_MODEL), lambda i: (i, 0)),
            pl.BlockSpec((tm, GLA_V), lambda i: (i, 0)),
            pl.BlockSpec((tm, ATT_Q), lambda i: (i, 0)),
            pl.BlockSpec((tm, D_MODEL), lambda i: (i, COL_ZA * LANE // D_MODEL)),
            pl.BlockSpec((tm, D_MODEL), lambda i: (i, COL_ZB * LANE // D_MODEL)),
            pl.BlockSpec((GLA_V, D_MODEL), full),
            pl.BlockSpec((ATT_Q, D_MODEL), full),
            pl.BlockSpec((D_MODEL, D_MODEL), full),
            pl.BlockSpec((1, D_MODEL), full),
            pl.BlockSpec((N_EXPERTS, D_MODEL), full),
        ],
        out_specs=[
            pl.BlockSpec((tm, D_MODEL), lambda i: (i, 0)),
            pl.BlockSpec((tm, D_MODEL), lambda i: (i, 0)),
            pl.BlockSpec((N_EXPERTS, tm), lambda i: (0, i)),
        ],
        out_shape=[
            jax.ShapeDtypeStruct((T, D_MODEL), F32),
            jax.ShapeDtypeStruct((T, D_MODEL), BF16),
            jax.ShapeDtypeStruct((N_EXPERTS, T), F32),
        ],
        compiler_params=_params("parallel"),
        name="merge",
    )(x, o_a, o_b, proj, proj, w_oa, w_ob, w_out, nf, w_rt)


def _lane_cumsum(x):
    n = x.shape[1]
    lane = lax.broadcasted_iota(I32, x.shape, 1)
    shift = 1
    while shift < n:
        x = x + jnp.where(lane >= shift, pltpu.roll(x, shift, 1), 0)
        shift *= 2
    return x


def _select_kernel(aff_ref, slot_ref, incl_ref, *, cap):
    bits = pltpu.bitcast(aff_ref[...], I32)

    def body(i, thr):
        cand = thr | jnp.left_shift(jnp.int32(1), 30 - i)
        cnt = jnp.sum((bits >= cand).astype(I32), axis=1, keepdims=True)
        return jnp.where(cnt >= cap, cand, thr)

    thr = lax.fori_loop(0, 31, body, jnp.zeros((bits.shape[0], 1), I32))

    aff = aff_ref[...]
    lo = pltpu.bitcast(jnp.maximum(thr - 1, 0), F32)
    hi = pltpu.bitcast(jnp.maximum(thr + 2, MIN_NORMAL_F32_BITS), F32)

    def refine(_, bracket):
        lo, hi = bracket
        mid = 0.5 * (lo + hi)
        enough = jnp.sum((aff >= mid).astype(I32), axis=1, keepdims=True) >= cap
        return jnp.where(enough, mid, lo), jnp.where(enough, hi, mid)

    lo, hi = lax.fori_loop(0, SELECT_REFINE_STEPS, refine, (lo, hi))
    above = aff >= hi
    need = cap - jnp.sum(above.astype(I32), axis=1, keepdims=True)
    ties = (aff >= lo) & (aff < hi)
    sel = above | (ties & (_lane_cumsum(ties.astype(I32)) <= need))
    incl = _lane_cumsum(sel.astype(I32))
    slot_ref[...] = jnp.where(sel, incl - 1, -1)
    incl_ref[...] = incl


def _select(aff_t, cap):
    E, T = aff_t.shape
    full = lambda: (0, 0)
    slot, incl = pl.pallas_call(
        functools.partial(_select_kernel, cap=cap),
        in_specs=[pl.BlockSpec((E, T), full)],
        out_specs=[pl.BlockSpec((E, T), full), pl.BlockSpec((E, T), full)],
        out_shape=[jax.ShapeDtypeStruct((E, T), I32), jax.ShapeDtypeStruct((E, T), I32)],
        compiler_params=pltpu.CompilerParams(vmem_limit_bytes=VMEM_LIMIT),
        name="select",
    )(aff_t)
    def before(step):
        return jnp.concatenate([jnp.zeros((E, 1), I32), incl[:, step - 1::step]], axis=1)

    return slot, before(GATHER_TC), before(ROUTE_TB)


def _gather_kernel(offs_ref, slot_ref, gate_ref, h_ref, xe_ref, gc_ref, acc_ref, g_ref, ptr_ref):
    e = pl.program_id(0)
    base = pl.program_id(1) * GATHER_WIN

    first = lax.while_loop(lambda c: offs_ref[e, c + 1] <= base, lambda c: c + 1,
                           jnp.where(base == 0, 0, ptr_ref[0]))
    ptr_ref[0] = first
    stop = lax.while_loop(lambda c: offs_ref[e, c] < base + GATHER_WIN, lambda c: c + 1, first)

    def picked(slots, gates, rows):
        want = lax.broadcasted_iota(I32, (GATHER_WIN, slots.shape[1]), 0) + base
        hit = slots == want
        return (_dot(jnp.where(hit, 1.0, 0.0).astype(BF16), h_ref[rows, :]),
                jnp.sum(jnp.where(hit, gates, 0.0), axis=1, keepdims=True))

    c0 = jnp.minimum(first, slot_ref.shape[1] - GATHER_SPAN)
    span = lambda ref: jnp.concatenate([ref[0, pl.ds(c0 + j, 1), :] for j in range(GATHER_SPAN)], axis=1)
    acc_ref[...], g_ref[...] = picked(span(slot_ref), span(gate_ref),
                                      pl.ds(pl.multiple_of(c0 * GATHER_TC, GATHER_TC), GATHER_SPAN * GATHER_TC))

    def body(c, carry):
        acc, g = picked(slot_ref[0, pl.ds(c, 1), :], gate_ref[0, pl.ds(c, 1), :],
                        pl.ds(pl.multiple_of(c * GATHER_TC, GATHER_TC), GATHER_TC))
        acc_ref[...] += acc
        g_ref[...] += g
        return carry

    lax.fori_loop(c0 + GATHER_SPAN, stop, body, 0)
    xe_ref[0] = acc_ref[...].astype(xe_ref.dtype)
    gc_ref[0] = g_ref[...]


def _gather(offs, slot, aff_t, h2, cap):
    E, T = slot.shape
    n_chunks = T // GATHER_TC
    assert cap % GATHER_WIN == 0 and T % GATHER_TC == 0 and n_chunks >= GATHER_SPAN
    row_spec = pl.BlockSpec((1, n_chunks, GATHER_TC), lambda e, w, offs: (e, 0, 0))
    return pl.pallas_call(
        _gather_kernel,
        grid_spec=pltpu.PrefetchScalarGridSpec(
            num_scalar_prefetch=1,
            grid=(E, cap // GATHER_WIN),
            in_specs=[row_spec, row_spec,
                      pl.BlockSpec((T, D_MODEL), lambda e, w, offs: (0, 0), pipeline_mode=pl.Buffered(1))],
            out_specs=[
                pl.BlockSpec((1, GATHER_WIN, D_MODEL), lambda e, w, offs: (e, w, 0)),
                pl.BlockSpec((1, GATHER_WIN, 1), lambda e, w, offs: (e, w, 0)),
            ],
            scratch_shapes=[pltpu.VMEM((GATHER_WIN, D_MODEL), F32), pltpu.VMEM((GATHER_WIN, 1), F32),
                            pltpu.SMEM((1,), I32)],
        ),
        out_shape=[
            jax.ShapeDtypeStruct((E, cap, D_MODEL), BF16),
            jax.ShapeDtypeStruct((E, cap, 1), F32),
        ],
        compiler_params=_params("arbitrary", "arbitrary"),
        name="gather",
    )(offs, slot.reshape(E, n_chunks, GATHER_TC), aff_t.reshape(E, n_chunks, GATHER_TC), h2)


def _scatter_window(offs_ref, e, t, cap, level):
    rows, align = level
    return pl.multiple_of(jnp.minimum((offs_ref[e, t] // align) * align, cap - rows), align)


def _scatter_copy(offs_ref, y_hbm, ybuf, sem, e, t, buf, cap, level):
    start = _scatter_window(offs_ref, e, t, cap, level)
    rows = level[0]
    return pltpu.make_async_copy(y_hbm.at[e, pl.ds(start, rows), :],
                                 ybuf.at[buf, pl.ds(e * rows, rows), :], sem.at[buf])


def _scatter_fits(offs_ref, t, level):
    rows, align = level
    fits = True
    for e in range(N_EXPERTS):
        off = offs_ref[e, t]
        fits = fits & ((off % align) + (offs_ref[e, t + 1] - off) <= rows)
    return fits


def _scatter_kernel(offs_ref, slot_ref, x1_ref, y_hbm, o_ref, ybuf, sem, *, cap, levels):
    t = pl.program_id(0)
    buf = t % 2

    def by_level(tt, fn):
        taken = False
        for level in levels[:-1]:
            use = jnp.logical_not(taken) & _scatter_fits(offs_ref, tt, level)
            pl.when(use)(functools.partial(fn, level))
            taken = taken | use
        pl.when(jnp.logical_not(taken))(functools.partial(fn, levels[-1]))

    def start(tt, b, level):
        for e in range(N_EXPERTS):
            _scatter_copy(offs_ref, y_hbm, ybuf, sem, e, tt, b, cap, level).start()

    @pl.when(t == 0)
    def _():
        by_level(t, functools.partial(start, t, buf))

    @pl.when(t + 1 < pl.num_programs(0))
    def _():
        by_level(t + 1, functools.partial(start, t + 1, 1 - buf))

    def finish(level):
        rows = level[0]
        lane = lax.broadcasted_iota(I32, (ROUTE_TB, rows), 1)
        hits = []
        for e in range(N_EXPERTS):
            rel = slot_ref[:, e:e + 1] - _scatter_window(offs_ref, e, t, cap, level)
            hits.append(jnp.where(rel == lane, 1.0, 0.0).astype(BF16))
        onehot = jnp.concatenate(hits, axis=1)
        for e in range(N_EXPERTS):
            _scatter_copy(offs_ref, y_hbm, ybuf, sem, e, t, buf, cap, level).wait()
        o_ref[...] = x1_ref[...] + _dot(onehot, ybuf[buf, pl.ds(0, N_EXPERTS * rows), :])

    by_level(t, finish)


def _scatter(offs, slot, x1, ye):
    E, T = slot.shape
    cap = ye.shape[1]
    levels = tuple((min(r, cap), a) for r, a in SCATTER_LEVELS)
    rows = levels[-1][0]
    return pl.pallas_call(
        functools.partial(_scatter_kernel, cap=cap, levels=levels),
        grid_spec=pltpu.PrefetchScalarGridSpec(
            num_scalar_prefetch=1,
            grid=(T // ROUTE_TB,),
            in_specs=[
                pl.BlockSpec((ROUTE_TB, E), lambda t, offs: (t, 0)),
                pl.BlockSpec((ROUTE_TB, D_MODEL), lambda t, offs: (t, 0)),
                pl.BlockSpec(memory_space=pl.ANY),
            ],
            out_specs=pl.BlockSpec((ROUTE_TB, D_MODEL), lambda t, offs: (t, 0)),
            scratch_shapes=[pltpu.VMEM((2, E * rows, D_MODEL), BF16), pltpu.SemaphoreType.DMA((2,))],
        ),
        out_shape=jax.ShapeDtypeStruct((T, D_MODEL), F32),
        compiler_params=_params("arbitrary"),
        name="scatter",
    )(offs, slot.T, x1, ye)


def _ffn_kernel(xe_ref, gate_ref, wg_ref, wu_ref, wd_ref, o_ref, acc_ref, *, rows):
    f = pl.program_id(1)
    cap = xe_ref.shape[1]

    @pl.when(f == 0)
    def _():
        acc_ref[...] = jnp.zeros_like(acc_ref)

    wg = wg_ref[0].astype(BF16)
    wu = wu_ref[0].astype(BF16)

    def up(i):
        xe = xe_ref[0, i * rows:(i + 1) * rows, :]
        return _dot(xe, wg), _dot(xe, wu)

    nxt = up(0)
    wd = wd_ref[0].astype(BF16)
    for i in range(cap // rows):
        hg, hu = nxt
        if i + 1 < cap // rows:
            nxt = up(i + 1)
        hid = (hg * jax.nn.sigmoid(hg) * hu).astype(BF16)
        acc_ref[i * rows:(i + 1) * rows, :] += _dot(hid, wd)

    @pl.when(f == pl.num_programs(1) - 1)
    def _():
        o_ref[0] = (acc_ref[...] * gate_ref[0]).astype(o_ref.dtype)


def _ffn(xe, gate, w_g, w_u, w_d, *, tf=512, rows=1024):
    E, cap, _ = xe.shape
    rows = min(rows, cap)
    return pl.pallas_call(
        functools.partial(_ffn_kernel, rows=rows),
        grid=(E, EXPERT_FF // tf),
        in_specs=[
            pl.BlockSpec((1, cap, D_MODEL), lambda e, f: (e, 0, 0)),
            pl.BlockSpec((1, cap, 1), lambda e, f: (e, 0, 0)),
            pl.BlockSpec((1, D_MODEL, tf), lambda e, f: (e, 0, f)),
            pl.BlockSpec((1, D_MODEL, tf), lambda e, f: (e, 0, f)),
            pl.BlockSpec((1, tf, D_MODEL), lambda e, f: (e, f, 0)),
        ],
        out_specs=pl.BlockSpec((1, cap, D_MODEL), lambda e, f: (e, 0, 0)),
        out_shape=jax.ShapeDtypeStruct((E, cap, D_MODEL), BF16),
        scratch_shapes=[pltpu.VMEM((cap, D_MODEL), F32)],
        compiler_params=_params("parallel", "arbitrary"),
        name="ffn",
    )(xe, gate, w_g, w_u, w_d)


def _prep_weights(w_in, w_gk_f, w_gk_b, q_norm, k_norm):
    pts = np.cumsum([GLA_QK, GLA_QK, GLA_V, GLA_V, GLA_RANK, GLA_RANK, ATT_Q, ATT_KV, ATT_KV, D_MODEL])
    gq, gk, gv, gg, r_f, r_b, aq, ak, av, z_a, z_b = jnp.split(w_in, pts, axis=-1)
    perm = np.concatenate([np.arange(0, ATT_HD, 2), np.arange(1, ATT_HD, 2)])
    perm_q = (np.arange(ATT_HEADS)[:, None] * ATT_HD + perm[None, :]).reshape(-1)
    perm_k = (np.arange(ATT_KV_HEADS)[:, None] * ATT_HD + perm[None, :]).reshape(-1)
    w_main = jnp.concatenate([gq, gk, gv, gg, z_a, z_b, aq[:, perm_q], ak[:, perm_k], av], axis=-1)
    w_r = jnp.pad(jnp.concatenate([r_f, r_b], axis=-1), ((0, 0), (0, LANE - 2 * GLA_RANK)))
    wf = jnp.pad(w_gk_f, ((0, LANE - GLA_RANK), (0, 0)))
    wb = jnp.pad(w_gk_b, ((GLA_RANK, LANE - 2 * GLA_RANK), (0, 0)))
    return (w_main.astype(BF16), w_r.astype(BF16), wf.astype(BF16), wb.astype(BF16),
            q_norm[perm][None, :], k_norm[perm][None, :])


def _layer(x3, norm_mix, w_main, w_r, wf, bf, wb, bb, gla_norm, qn, kn,
           w_oa, w_ob, w_out, norm_ffn, w_rt, w_e_gate, w_e_up, w_e_down):
    B, S, D = x3.shape
    T = B * S
    x = x3.reshape(T, D)
    proj, r = _inproj(x, norm_mix, w_r, w_main)
    proj3, r3 = proj.reshape(B, S, -1), r.reshape(B, S, LANE)
    o_f = _gla_dir(proj3, r3, wf, bf, True)
    o_a = _gla_dir(proj3, r3, wb, bb, False, o_f, gla_norm)
    cos, sin = _rope_tables(S)
    q, k, v = _qkprep(proj, cos, sin, qn, kn, S)
    o_b = _flash(q.reshape(B, S, ATT_Q), k.reshape(B, S, ATT_KV), v.reshape(B, S, 2 * ATT_KV))
    x1, h2, aff_t = _merge(x, o_a.reshape(T, GLA_V), o_b.reshape(T, ATT_Q), proj,
                           w_oa, w_ob, w_out, norm_ffn, w_rt)
    cap = CAPACITY_FACTOR * T // N_EXPERTS
    slot, offs_gather, offs_scatter = _select(aff_t, cap)
    xe, gate = _gather(offs_gather, slot, aff_t, h2, cap)
    ye = _ffn(xe, gate, w_e_gate, w_e_up, w_e_down)
    out = _scatter(offs_scatter, slot, x1, ye)
    return out.reshape(B, S, D)


def kernel(x_prompt, x_sample, norm_mix, w_in, w_gk_f, b_gk_f, w_gk_b, b_gk_b, gla_norm, q_norm, k_norm,
           w_o_gla, w_o_att, w_out, norm_ffn, w_router, w_e_gate, w_e_up, w_e_down):
    w_main, w_r, wf, wb, qn, kn = _prep_weights(w_in[0], w_gk_f[0], w_gk_b[0], q_norm[0], k_norm[0])
    args = (norm_mix[0][None, :], w_main, w_r, wf, b_gk_f[0][None, :], wb, b_gk_b[0][None, :],
            gla_norm[0][None, :], qn, kn,
            w_o_gla[0].astype(BF16), w_o_att[0].astype(BF16), w_out[0].astype(BF16),
            norm_ffn[0][None, :], w_router[0].T, w_e_gate[0], w_e_up[0], w_e_down[0])
    return (_layer(x_prompt, *args), _layer(x_sample, *args))
```

```python
import functools

import jax
import jax.numpy as jnp
import numpy as np
from jax import lax
from jax.experimental import pallas as pl
from jax.experimental.pallas import tpu as pltpu

D_MODEL = 1024
GRID_W = 64
GLA_HEADS = 4
GLA_DK = 128
GLA_DV = 256
GLA_RANK = 16
GLA_GATE_NORM = 16.0
GLA_CHUNK = 64
GLA_QK = GLA_HEADS * GLA_DK
GLA_V = GLA_HEADS * GLA_DV
ATT_HEADS = 8
ATT_KV_HEADS = 2
ATT_GROUP = ATT_HEADS // ATT_KV_HEADS
ATT_HD = 128
ATT_Q = ATT_HEADS * ATT_HD
ATT_KV = ATT_KV_HEADS * ATT_HD
ROPE_THETA = 10000.0
Q_BLOCK = 128
N_EXPERTS = 16
CAPACITY_FACTOR = 2
EXPERT_FF = 2048
EPS = 1e-6
LOG2_E = 1.4426950408889634

LANE = 128
VMEM_LIMIT = 56 * 1024 * 1024

BF16 = jnp.bfloat16
F32 = jnp.float32
I32 = jnp.int32

COL_GQ, COL_GK, COL_GV, COL_GG = 0, 4, 8, 16
COL_ZA, COL_ZB, COL_AQ, COL_AK, COL_AV = 24, 32, 40, 48, 50
N_MAIN_COLS = 52 * LANE

GLA_BLOCK = 256
GLA_SUB = GLA_BLOCK // GLA_CHUNK
ROUTE_TB = 256
ROUTE_WIN = 128
ROUTE_NWIN = ROUTE_TB // ROUTE_WIN + 1
SCATTER_LEVELS = ((ROUTE_WIN, ROUTE_WIN // 2), (ROUTE_TB, ROUTE_WIN), (ROUTE_NWIN * ROUTE_WIN, ROUTE_WIN))
FLASH_TK = 4096
FLASH_SCORES = 512 * 4096
MIN_NORMAL_F32_BITS = 0x00800000
SELECT_REFINE_STEPS = 32
GATHER_TC = 512
GATHER_WIN = 128
GATHER_SPAN = 4


def _nt_dot(a, b, **kw):
    return lax.dot_general(a, b, (((1,), (1,)), ((), ())), preferred_element_type=F32, **kw)


def _tn_dot(a, b):
    return lax.dot_general(a, b, (((0,), (0,)), ((), ())), preferred_element_type=F32)


def _dot(a, b):
    return jnp.dot(a, b, preferred_element_type=F32)


def _params(*sem):
    return pltpu.CompilerParams(dimension_semantics=sem, vmem_limit_bytes=VMEM_LIMIT)


def _inproj_kernel(x_ref, g_ref, wr_ref, w_ref, o_ref, r_ref, hn_ref):
    @pl.when(pl.program_id(1) == 0)
    def _():
        x = x_ref[...]
        ms = jnp.mean(x * x, axis=-1, keepdims=True)
        hn = (x * lax.rsqrt(ms + EPS) * g_ref[...]).astype(BF16)
        hn_ref[...] = hn
        r_ref[...] = _dot(hn, wr_ref[...])

    o_ref[...] = _dot(hn_ref[...], w_ref[...])


def _inproj(x, g, w_r, w_main, *, tm=1024, n_col_tiles=4):
    T = x.shape[0]
    tn = N_MAIN_COLS // n_col_tiles
    return pl.pallas_call(
        _inproj_kernel,
        grid=(T // tm, n_col_tiles),
        in_specs=[
            pl.BlockSpec((tm, D_MODEL), lambda i, j: (i, 0)),
            pl.BlockSpec((1, D_MODEL), lambda i, j: (0, 0)),
            pl.BlockSpec((D_MODEL, LANE), lambda i, j: (0, 0)),
            pl.BlockSpec((D_MODEL, tn), lambda i, j: (0, j)),
        ],
        out_specs=[
            pl.BlockSpec((tm, tn), lambda i, j: (i, j)),
            pl.BlockSpec((tm, LANE), lambda i, j: (i, 0)),
        ],
        out_shape=[
            jax.ShapeDtypeStruct((T, N_MAIN_COLS), F32),
            jax.ShapeDtypeStruct((T, LANE), F32),
        ],
        scratch_shapes=[pltpu.VMEM((tm, D_MODEL), BF16)],
        compiler_params=_params("parallel", "arbitrary"),
        name="inproj",
    )(x, g, w_r, w_main)


def _log_sigmoid(x):
    return jnp.minimum(x, 0.0) - jnp.log(1.0 + jnp.exp(-jnp.abs(x)))


def _rows(parts):
    return jnp.concatenate([jnp.broadcast_to(p, (GLA_CHUNK, p.shape[-1])) for p in parts], axis=0)


def _gla_masks(fwd):
    n = GLA_BLOCK
    row = lax.broadcasted_iota(I32, (n, n), 0)
    col = lax.broadcasted_iota(I32, (n, n), 1)
    same = (row // GLA_CHUNK) == (col // GLA_CHUNK)
    tri = same & ((col <= row) if fwd else (col >= row))
    dist = (row // GLA_CHUNK - col // GLA_CHUNK) * (1 if fwd else -1)
    return tri, dist


def _gla_prep(q, k, r, w, bias, tri, fwd):
    C = GLA_CHUNK
    lg = _log_sigmoid(_dot(r, w) + bias) * (1.0 / GLA_GATE_NORM)
    hi = lg.astype(BF16)
    rem = lg - hi.astype(F32)
    mid = rem.astype(BF16)
    lo = (rem - mid.astype(F32)).astype(BF16)
    tri_b = jnp.where(tri, 1.0, 0.0).astype(BF16)
    b = _dot(tri_b, hi) + _dot(tri_b, mid) + _dot(tri_b, lo)
    last = (lambda s: s * C + C - 1) if fwd else (lambda s: s * C)
    tot = [b[last(s):last(s) + 1, :] for s in range(GLA_SUB)]
    zero = jnp.zeros_like(tot[0])
    before = [zero, tot[0], tot[0] + tot[1], tot[0] + tot[1] + tot[2]]
    after = [tot[1] + tot[2] + tot[3], tot[2] + tot[3], tot[3], zero]
    lead, trail = (before, after) if fwd else (after, before)

    q_t = q * jnp.exp(b) * (GLA_DK ** -0.5)
    k_t = (k * jnp.exp(-b)).astype(BF16)
    k_hat = k * jnp.exp(_rows(tot) - b)
    q_in = (q_t * _rows([jnp.exp(x) for x in lead])).astype(BF16)
    k_out = (k_hat * _rows([jnp.exp(x) for x in trail])).astype(BF16)
    k_hat = k_hat.astype(BF16)

    e1, e2, e12 = jnp.exp(tot[1]), jnp.exp(tot[2]), jnp.exp(tot[1] + tot[2])
    if fwd:
        q1, q2, q3 = q_t[C:], q_t[2 * C:], q_t[3 * C:]
    else:
        q1, q2, q3 = q_t[:3 * C], q_t[:2 * C], q_t[:C]
    q2 = q2 * jnp.concatenate([jnp.broadcast_to(e1, (C, GLA_DK)), jnp.broadcast_to(e2, (C, GLA_DK))], axis=0)
    q3 = q3 * e12
    q_far = jnp.concatenate([q1, q2, q3], axis=0).astype(BF16)
    decay = jnp.exp(tot[0] + tot[1] + tot[2] + tot[3])
    return q_t.astype(BF16), k_t, q_far, k_hat, q_in, k_out, decay


def _gla_finish(ops, v, st, tri, dist, fwd):
    C = GLA_CHUNK
    q_t, k_t, q_far, k_hat, q_in, k_out, decay = ops
    y = _nt_dot(q_far, k_hat)
    y1, y2, y3 = y[:3 * C], y[3 * C:5 * C], y[5 * C:]
    z = lambda rows: jnp.zeros((rows, GLA_BLOCK), F32)
    if fwd:
        y1, y2, y3 = (jnp.concatenate([z(C), y1], 0), jnp.concatenate([z(2 * C), y2], 0),
                      jnp.concatenate([z(3 * C), y3], 0))
    else:
        y1, y2, y3 = (jnp.concatenate([y1, z(C)], 0), jnp.concatenate([y2, z(2 * C)], 0),
                      jnp.concatenate([y3, z(3 * C)], 0))
    x0 = _nt_dot(q_t, k_t)
    a = jnp.where(tri, x0, jnp.where(dist == 1, y1, jnp.where(dist == 2, y2, jnp.where(dist == 3, y3, 0.0))))
    o = _dot(a.astype(BF16), v) + _nt_dot(q_in, st.astype(BF16))
    st_new = st * decay + _tn_dot(v, k_out)
    return o, st_new


def _gla_dir_kernel(*refs, L, fwd):
    if fwd:
        q_ref, k_ref, v_ref, r_ref, w_ref, b_ref, o_ref, st_ref = refs
    else:
        q_ref, k_ref, v_ref, r_ref, w_ref, b_ref, g_ref, of_ref, gn_ref, o_ref, st_ref = refs
    nb = L // GLA_BLOCK

    @pl.when(pl.program_id(1) == 0)
    def _():
        st_ref[...] = jnp.zeros_like(st_ref)

    tri, dist = _gla_masks(fwd)

    def body(i, carry):
        blk = i if fwd else nb - 1 - i
        sl = pl.ds(pl.multiple_of(blk * GLA_BLOCK, GLA_BLOCK), GLA_BLOCK)
        r = r_ref[0, sl, :].astype(BF16)

        def prep(h):
            ks = slice(h * GLA_DK, (h + 1) * GLA_DK)
            return _gla_prep(q_ref[0, sl, ks], k_ref[0, sl, ks], r, w_ref[:, ks], b_ref[:, ks], tri, fwd)

        nxt = prep(0)
        for h in range(GLA_HEADS):
            vs = slice(h * GLA_DV, (h + 1) * GLA_DV)
            ops = nxt
            if h + 1 < GLA_HEADS:
                nxt = prep(h + 1)
            o, st = _gla_finish(ops, v_ref[0, sl, vs].astype(BF16), st_ref[h], tri, dist, fwd)
            st_ref[h] = st
            if fwd:
                o_ref[0, sl, vs] = o
            else:
                o = o + of_ref[0, sl, vs]
                ms = jnp.mean(o * o, axis=-1, keepdims=True)
                o = o * lax.rsqrt(ms + EPS) * gn_ref[...]
                g = g_ref[0, sl, vs]
                o_ref[0, sl, vs] = (o * (g * jax.nn.sigmoid(g))).astype(o_ref.dtype)
        return carry

    lax.fori_loop(0, nb, body, 0)


def _gla_dir(proj3, r3, w, bias, fwd, o_f=None, gn=None, *, L=512):
    B, S, _ = proj3.shape
    nl = S // L
    pos = (lambda n: n) if fwd else (lambda n: nl - 1 - n)
    full = lambda b, n: (0, 0)
    in_specs = [
        pl.BlockSpec((1, L, GLA_QK), lambda b, n: (b, pos(n), COL_GQ * LANE // GLA_QK)),
        pl.BlockSpec((1, L, GLA_QK), lambda b, n: (b, pos(n), COL_GK * LANE // GLA_QK)),
        pl.BlockSpec((1, L, GLA_V), lambda b, n: (b, pos(n), COL_GV * LANE // GLA_V)),
        pl.BlockSpec((1, L, LANE), lambda b, n: (b, pos(n), 0)),
        pl.BlockSpec((LANE, GLA_QK), full),
        pl.BlockSpec((1, GLA_QK), full),
    ]
    args = [proj3, proj3, proj3, r3, w, bias]
    if not fwd:
        in_specs += [
            pl.BlockSpec((1, L, GLA_V), lambda b, n: (b, pos(n), COL_GG * LANE // GLA_V)),
            pl.BlockSpec((1, L, GLA_V), lambda b, n: (b, pos(n), 0)),
            pl.BlockSpec((1, GLA_DV), full),
        ]
        args += [proj3, o_f, gn]
    return pl.pallas_call(
        functools.partial(_gla_dir_kernel, L=L, fwd=fwd),
        grid=(B, nl),
        in_specs=in_specs,
        out_specs=pl.BlockSpec((1, L, GLA_V), lambda b, n: (b, pos(n), 0)),
        out_shape=jax.ShapeDtypeStruct((B, S, GLA_V), F32 if fwd else BF16),
        scratch_shapes=[pltpu.VMEM((GLA_HEADS, GLA_DV, GLA_DK), F32)],
        compiler_params=_params("parallel", "arbitrary"),
        name="gla_fwd" if fwd else "gla_bwd",
    )(*args)


def _rope_tables(S):
    rows = S // GRID_W
    pos_r = np.repeat(np.arange(rows), GRID_W).astype(np.float32)
    pos_c = np.tile(np.arange(GRID_W), rows).astype(np.float32)
    half = ATT_HD // 2
    inv = jnp.asarray(ROPE_THETA, F32) ** (-jnp.arange(0, half, 2, dtype=F32) / half)
    ang = jnp.concatenate([pos_r[:, None] * inv, pos_c[:, None] * inv], axis=-1)
    cos, sin = jnp.cos(ang), jnp.sin(ang)
    return jnp.concatenate([cos, cos], axis=-1), jnp.concatenate([-sin, sin], axis=-1)


def _qkprep_kernel(aq_ref, akv_ref, cos_ref, sin_ref, qn_ref, kn_ref, q_out, k_out, v_out):
    cos = cos_ref[...]
    sin = sin_ref[...]

    def norm_rope(x, w, scale):
        ms = jnp.mean(x * x, axis=-1, keepdims=True)
        y = x * lax.rsqrt(ms + EPS) * w
        return (y * cos + pltpu.roll(y, ATT_HD // 2, 1) * sin) * scale

    for h in range(ATT_HEADS):
        sl = slice(h * ATT_HD, (h + 1) * ATT_HD)
        q_out[:, sl] = norm_rope(aq_ref[:, sl], qn_ref[...], ATT_HD ** -0.5 * LOG2_E).astype(BF16)
    for h in range(ATT_KV_HEADS):
        sl = slice(h * ATT_HD, (h + 1) * ATT_HD)
        k_out[:, sl] = norm_rope(akv_ref[:, sl], kn_ref[...], 1.0).astype(BF16)
    for h in range(ATT_KV_HEADS):
        v_out[:, 2 * h * ATT_HD:(2 * h + 1) * ATT_HD] = akv_ref[:, ATT_KV + h * ATT_HD:ATT_KV + (h + 1) * ATT_HD].astype(BF16)
        v_out[:, (2 * h + 1) * ATT_HD:(2 * h + 2) * ATT_HD] = jnp.ones((v_out.shape[0], ATT_HD), BF16)


def _qkprep(proj, cos, sin, qn, kn, S, *, tm=512):
    T = proj.shape[0]
    ns = S // tm
    return pl.pallas_call(
        _qkprep_kernel,
        grid=(T // tm,),
        in_specs=[
            pl.BlockSpec((tm, ATT_Q), lambda i: (i, COL_AQ * LANE // ATT_Q)),
            pl.BlockSpec((tm, 2 * ATT_KV), lambda i: (i, COL_AK * LANE // (2 * ATT_KV))),
            pl.BlockSpec((tm, ATT_HD), lambda i: (i % ns, 0)),
            pl.BlockSpec((tm, ATT_HD), lambda i: (i % ns, 0)),
            pl.BlockSpec((1, ATT_HD), lambda i: (0, 0)),
            pl.BlockSpec((1, ATT_HD), lambda i: (0, 0)),
        ],
        out_specs=[
            pl.BlockSpec((tm, ATT_Q), lambda i: (i, 0)),
            pl.BlockSpec((tm, ATT_KV), lambda i: (i, 0)),
            pl.BlockSpec((tm, 2 * ATT_KV), lambda i: (i, 0)),
        ],
        out_shape=[
            jax.ShapeDtypeStruct((T, ATT_Q), BF16),
            jax.ShapeDtypeStruct((T, ATT_KV), BF16),
            jax.ShapeDtypeStruct((T, 2 * ATT_KV), BF16),
        ],
        compiler_params=_params("parallel"),
        name="qkprep",
    )(proj, proj, cos, sin, qn, kn)


def _flash_kernel(q_ref, k_ref, v_ref, o_ref, m_ref, acc_ref, *, tq):
    ki = pl.program_id(3)
    tk = k_ref.shape[1]

    @pl.when(ki == 0)
    def _():
        m_ref[...] = jnp.full_like(m_ref, -jnp.inf)
        acc_ref[...] = jnp.zeros_like(acc_ref)

    k = k_ref[0]
    v = v_ref[0]

    def scores(g):
        return _nt_dot(q_ref[0, :, g * ATT_HD:(g + 1) * ATT_HD], k)

    def update(g, s):
        part = s[:, :LANE]
        for j in range(1, tk // LANE):
            part = jnp.maximum(part, s[:, j * LANE:(j + 1) * LANE])
        m = m_ref[g]
        m_new = jnp.maximum(m, jnp.max(part, axis=-1, keepdims=True))
        p = jnp.exp2(s - m_new).astype(BF16)
        acc_ref[g] = jnp.exp2(m - m_new) * acc_ref[g] + _dot(p, v)
        m_ref[g] = m_new

    s_next = scores(0)
    for g in range(ATT_GROUP):
        s = s_next
        if g + 1 < ATT_GROUP:
            s_next = scores(g + 1)
        update(g, s)

    @pl.when(ki == pl.num_programs(3) - 1)
    def _():
        for g in range(ATT_GROUP):
            acc = acc_ref[g]
            out = acc[:, :ATT_HD] / acc[:, ATT_HD:ATT_HD + 1]
            for u in range(tq // Q_BLOCK):
                rows = slice(u * Q_BLOCK, (u + 1) * Q_BLOCK)
                o_ref[0, g, :, u * ATT_HD:(u + 1) * ATT_HD] = out[rows, :].astype(o_ref.dtype)


def _flash(q3, k3, v3):
    B, S, _ = q3.shape
    gw = ATT_GROUP * ATT_HD
    tk = min(FLASH_TK, S)
    tq = min(FLASH_SCORES // tk, S)
    n_qt = S // tq
    return pl.pallas_call(
        functools.partial(_flash_kernel, tq=tq),
        grid=(B, ATT_KV_HEADS, S // tq, S // tk),
        in_specs=[
            pl.BlockSpec((1, tq, gw), lambda b, h, qi, ki: (b, qi, h)),
            pl.BlockSpec((1, tk, ATT_HD), lambda b, h, qi, ki: (b, ki, h)),
            pl.BlockSpec((1, tk, 2 * ATT_HD), lambda b, h, qi, ki: (b, ki, h)),
        ],
        out_specs=pl.BlockSpec((1, ATT_GROUP, Q_BLOCK, tq), lambda b, h, qi, ki: (b, 0, 0, h * n_qt + qi)),
        out_shape=jax.ShapeDtypeStruct((B, ATT_GROUP, Q_BLOCK, ATT_KV_HEADS * S), BF16),
        scratch_shapes=[
            pltpu.VMEM((ATT_GROUP, tq, 1), F32),
            pltpu.VMEM((ATT_GROUP, tq, 2 * ATT_HD), F32),
        ],
        compiler_params=_params("parallel", "parallel", "parallel", "arbitrary"),
        name="flash",
    )(q3, k3, v3)


def _merge_kernel(x_ref, oa_ref, ob_ref, za_ref, zb_ref, woa_ref, wob_ref, wout_ref, nf_ref, wr_ref,
                  x1_ref, h2_ref, aff_ref):
    ya = _dot(oa_ref[...], woa_ref[...])
    yb = _dot(ob_ref[...], wob_ref[...])
    merged = jax.nn.sigmoid(za_ref[...]) * ya + jax.nn.sigmoid(zb_ref[...]) * yb
    x1 = x_ref[...] + _dot(merged.astype(BF16), wout_ref[...])
    x1_ref[...] = x1
    ms = jnp.mean(x1 * x1, axis=-1, keepdims=True)
    h2 = x1 * lax.rsqrt(ms + EPS) * nf_ref[...]
    h2_ref[...] = h2.astype(BF16)
    logits = _nt_dot(wr_ref[...], h2, precision=lax.Precision.HIGHEST)
    e = jnp.exp(logits - jnp.max(logits, axis=0, keepdims=True))
    aff_ref[...] = e / jnp.sum(e, axis=0, keepdims=True)


def _merge(x, o_a, o_b, proj, w_oa, w_ob, w_out, nf, w_rt, *, tm=512):
    T = x.shape[0]
    full = lambda i: (0, 0)
    return pl.pallas_call(
        _merge_kernel,
        grid=(T // tm,),
        in_specs=[
            pl.BlockSpec((tm, D_MODEL), lambda i: (i, 0)),
            pl.BlockSpec((tm, GLA_V), lambda i: (i, 0)),
            pl.BlockSpec((tm, ATT_Q), lambda i: (i, 0)),
            pl.BlockSpec((tm, D_MODEL), lambda i: (i, COL_ZA * LANE // D_MODEL)),
            pl.BlockSpec((tm, D_MODEL), lambda i: (i, COL_ZB * LANE // D_MODEL)),
            pl.BlockSpec((GLA_V, D_MODEL), full),
            pl.BlockSpec((ATT_Q, D_MODEL), full),
            pl.BlockSpec((D_MODEL, D_MODEL), full),
            pl.BlockSpec((1, D_MODEL), full),
            pl.BlockSpec((N_EXPERTS, D_MODEL), full),
        ],
        out_specs=[
            pl.BlockSpec((tm, D_MODEL), lambda i: (i, 0)),
            pl.BlockSpec((tm, D_MODEL), lambda i: (i, 0)),
            pl.BlockSpec((N_EXPERTS, tm), lambda i: (0, i)),
        ],
        out_shape=[
            jax.ShapeDtypeStruct((T, D_MODEL), F32),
            jax.ShapeDtypeStruct((T, D_MODEL), BF16),
            jax.ShapeDtypeStruct((N_EXPERTS, T), F32),
        ],
        compiler_params=_params("parallel"),
        name="merge",
    )(x, o_a, o_b, proj, proj, w_oa, w_ob, w_out, nf, w_rt)


def _lane_cumsum(x):
    n = x.shape[1]
    lane = lax.broadcasted_iota(I32, x.shape, 1)
    shift = 1
    while shift < n:
        x = x + jnp.where(lane >= shift, pltpu.roll(x, shift, 1), 0)
        shift *= 2
    return x


def _select_kernel(aff_ref, slot_ref, incl_ref, *, cap):
    aff = aff_ref[...]

    def at_least_cap(value):
        return jnp.sum((aff >= value).astype(I32), axis=1, keepdims=True) >= cap

    def body(i, thr):
        cand = thr | jnp.left_shift(jnp.int32(1), 30 - i)
        return jnp.where(at_least_cap(pltpu.bitcast(cand, F32)), cand, thr)

    thr = lax.fori_loop(0, 31, body, jnp.zeros((aff.shape[0], 1), I32))

    lo = pltpu.bitcast(thr, F32).astype(aff.dtype)
    hi = pltpu.bitcast(jnp.maximum(thr + 1, MIN_NORMAL_F32_BITS), F32).astype(aff.dtype)

    def refine(_, bracket):
        lo, hi = bracket
        mid = 0.5 * (lo + hi)
        enough = at_least_cap(mid)
        return jnp.where(enough, mid, lo), jnp.where(enough, hi, mid)

    lo, hi = lax.fori_loop(0, SELECT_REFINE_STEPS, refine, (lo, hi))
    above = aff >= hi
    need = cap - jnp.sum(above.astype(I32), axis=1, keepdims=True)
    ties = (aff >= lo) & (aff < hi)
    sel = above | (ties & (_lane_cumsum(ties.astype(I32)) <= need))
    incl = _lane_cumsum(sel.astype(I32))
    slot_ref[...] = jnp.where(sel, incl - 1, -1)
    incl_ref[...] = incl


def _select(aff_t, cap):
    E, T = aff_t.shape
    full = lambda: (0, 0)
    slot, incl = pl.pallas_call(
        functools.partial(_select_kernel, cap=cap),
        in_specs=[pl.BlockSpec((E, T), full)],
        out_specs=[pl.BlockSpec((E, T), full), pl.BlockSpec((E, T), full)],
        out_shape=[jax.ShapeDtypeStruct((E, T), I32), jax.ShapeDtypeStruct((E, T), I32)],
        compiler_params=pltpu.CompilerParams(vmem_limit_bytes=VMEM_LIMIT),
        name="select",
    )(aff_t)
    def before(step):
        return jnp.concatenate([jnp.zeros((E, 1), I32), incl[:, step - 1::step]], axis=1)

    return slot, before(GATHER_TC), before(ROUTE_TB)


def _gather_kernel(offs_ref, slot_ref, gate_ref, h_ref, xe_ref, gc_ref, acc_ref, g_ref, ptr_ref):
    e = pl.program_id(0)
    base = pl.program_id(1) * GATHER_WIN

    first = lax.while_loop(lambda c: offs_ref[e, c + 1] <= base, lambda c: c + 1,
                           jnp.where(base == 0, 0, ptr_ref[0]))
    ptr_ref[0] = first
    stop = lax.while_loop(lambda c: offs_ref[e, c] < base + GATHER_WIN, lambda c: c + 1, first)

    def picked(slots, gates, rows):
        want = lax.broadcasted_iota(I32, (GATHER_WIN, slots.shape[1]), 0) + base
        hit = slots == want
        return (_dot(jnp.where(hit, 1.0, 0.0).astype(BF16), h_ref[rows, :]),
                jnp.sum(jnp.where(hit, gates, 0.0), axis=1, keepdims=True))

    c0 = jnp.minimum(first, slot_ref.shape[1] - GATHER_SPAN)
    span = lambda ref: jnp.concatenate([ref[0, pl.ds(c0 + j, 1), :] for j in range(GATHER_SPAN)], axis=1)
    acc_ref[...], g_ref[...] = picked(span(slot_ref), span(gate_ref),
                                      pl.ds(pl.multiple_of(c0 * GATHER_TC, GATHER_TC), GATHER_SPAN * GATHER_TC))

    def body(c, carry):
        acc, g = picked(slot_ref[0, pl.ds(c, 1), :], gate_ref[0, pl.ds(c, 1), :],
                        pl.ds(pl.multiple_of(c * GATHER_TC, GATHER_TC), GATHER_TC))
        acc_ref[...] += acc
        g_ref[...] += g
        return carry

    lax.fori_loop(c0 + GATHER_SPAN, stop, body, 0)
    xe_ref[0] = acc_ref[...].astype(xe_ref.dtype)
    gc_ref[0] = g_ref[...]


def _gather(offs, slot, aff_t, h2, cap):
    E, T = slot.shape
    n_chunks = T // GATHER_TC
    assert cap % GATHER_WIN == 0 and T % GATHER_TC == 0 and n_chunks >= GATHER_SPAN
    row_spec = pl.BlockSpec((1, n_chunks, GATHER_TC), lambda e, w, offs: (e, 0, 0))
    return pl.pallas_call(
        _gather_kernel,
        grid_spec=pltpu.PrefetchScalarGridSpec(
            num_scalar_prefetch=1,
            grid=(E, cap // GATHER_WIN),
            in_specs=[row_spec, row_spec,
                      pl.BlockSpec((T, D_MODEL), lambda e, w, offs: (0, 0), pipeline_mode=pl.Buffered(1))],
            out_specs=[
                pl.BlockSpec((1, GATHER_WIN, D_MODEL), lambda e, w, offs: (e, w, 0)),
                pl.BlockSpec((1, GATHER_WIN, 1), lambda e, w, offs: (e, w, 0)),
            ],
            scratch_shapes=[pltpu.VMEM((GATHER_WIN, D_MODEL), F32), pltpu.VMEM((GATHER_WIN, 1), F32),
                            pltpu.SMEM((1,), I32)],
        ),
        out_shape=[
            jax.ShapeDtypeStruct((E, cap, D_MODEL), BF16),
            jax.ShapeDtypeStruct((E, cap, 1), F32),
        ],
        compiler_params=_params("arbitrary", "arbitrary"),
        name="gather",
    )(offs, slot.reshape(E, n_chunks, GATHER_TC), aff_t.reshape(E, n_chunks, GATHER_TC), h2)


def _scatter_window(offs_ref, e, t, cap, level):
    rows, align = level
    return pl.multiple_of(jnp.minimum((offs_ref[e, t] // align) * align, cap - rows), align)


def _scatter_copy(offs_ref, y_hbm, ybuf, sem, e, t, buf, cap, level):
    start = _scatter_window(offs_ref, e, t, cap, level)
    rows = level[0]
    return pltpu.make_async_copy(y_hbm.at[e, pl.ds(start, rows), :],
                                 ybuf.at[buf, pl.ds(e * rows, rows), :], sem.at[buf])


def _scatter_fits(offs_ref, t, level):
    rows, align = level
    fits = True
    for e in range(N_EXPERTS):
        off = offs_ref[e, t]
        fits = fits & ((off % align) + (offs_ref[e, t + 1] - off) <= rows)
    return fits


def _scatter_kernel(offs_ref, slot_ref, x1_ref, y_hbm, o_ref, ybuf, sem, *, cap, levels):
    t = pl.program_id(0)
    buf = t % 2

    def by_level(tt, fn):
        taken = False
        for level in levels[:-1]:
            use = jnp.logical_not(taken) & _scatter_fits(offs_ref, tt, level)
            pl.when(use)(functools.partial(fn, level))
            taken = taken | use
        pl.when(jnp.logical_not(taken))(functools.partial(fn, levels[-1]))

    def start(tt, b, level):
        for e in range(N_EXPERTS):
            _scatter_copy(offs_ref, y_hbm, ybuf, sem, e, tt, b, cap, level).start()

    @pl.when(t == 0)
    def _():
        by_level(t, functools.partial(start, t, buf))

    @pl.when(t + 1 < pl.num_programs(0))
    def _():
        by_level(t + 1, functools.partial(start, t + 1, 1 - buf))

    def finish(level):
        rows = level[0]
        lane = lax.broadcasted_iota(I32, (ROUTE_TB, rows), 1)
        hits = []
        for e in range(N_EXPERTS):
            rel = slot_ref[:, e:e + 1] - _scatter_window(offs_ref, e, t, cap, level)
            hits.append(jnp.where(rel == lane, 1.0, 0.0).astype(BF16))
        onehot = jnp.concatenate(hits, axis=1)
        for e in range(N_EXPERTS):
            _scatter_copy(offs_ref, y_hbm, ybuf, sem, e, t, buf, cap, level).wait()
        o_ref[...] = x1_ref[...] + _dot(onehot, ybuf[buf, pl.ds(0, N_EXPERTS * rows), :])

    by_level(t, finish)


def _scatter(offs, slot, x1, ye):
    E, T = slot.shape
    cap = ye.shape[1]
    levels = tuple((min(r, cap), a) for r, a in SCATTER_LEVELS)
    rows = levels[-1][0]
    return pl.pallas_call(
        functools.partial(_scatter_kernel, cap=cap, levels=levels),
        grid_spec=pltpu.PrefetchScalarGridSpec(
            num_scalar_prefetch=1,
            grid=(T // ROUTE_TB,),
            in_specs=[
                pl.BlockSpec((ROUTE_TB, E), lambda t, offs: (t, 0)),
                pl.BlockSpec((ROUTE_TB, D_MODEL), lambda t, offs: (t, 0)),
                pl.BlockSpec(memory_space=pl.ANY),
            ],
            out_specs=pl.BlockSpec((ROUTE_TB, D_MODEL), lambda t, offs: (t, 0)),
            scratch_shapes=[pltpu.VMEM((2, E * rows, D_MODEL), BF16), pltpu.SemaphoreType.DMA((2,))],
        ),
        out_shape=jax.ShapeDtypeStruct((T, D_MODEL), F32),
        compiler_params=_params("arbitrary"),
        name="scatter",
    )(offs, slot.T, x1, ye)


def _ffn_kernel(xe_ref, gate_ref, wg_ref, wu_ref, wd_ref, o_ref, acc_ref, *, rows):
    f = pl.program_id(1)
    cap = xe_ref.shape[1]

    @pl.when(f == 0)
    def _():
        acc_ref[...] = jnp.zeros_like(acc_ref)

    wg = wg_ref[0].astype(BF16)
    wu = wu_ref[0].astype(BF16)

    def up(i):
        xe = xe_ref[0, i * rows:(i + 1) * rows, :]
        return _dot(xe, wg), _dot(xe, wu)

    nxt = up(0)
    wd = wd_ref[0].astype(BF16)
    for i in range(cap // rows):
        hg, hu = nxt
        if i + 1 < cap // rows:
            nxt = up(i + 1)
        hid = (hg * jax.nn.sigmoid(hg) * hu).astype(BF16)
        acc_ref[i * rows:(i + 1) * rows, :] += _dot(hid, wd)

    @pl.when(f == pl.num_programs(1) - 1)
    def _():
        o_ref[0] = (acc_ref[...] * gate_ref[0]).astype(o_ref.dtype)


def _ffn(xe, gate, w_g, w_u, w_d, *, tf=512, rows=1024):
    E, cap, _ = xe.shape
    rows = min(rows, cap)
    return pl.pallas_call(
        functools.partial(_ffn_kernel, rows=rows),
        grid=(E, EXPERT_FF // tf),
        in_specs=[
            pl.BlockSpec((1, cap, D_MODEL), lambda e, f: (e, 0, 0)),
            pl.BlockSpec((1, cap, 1), lambda e, f: (e, 0, 0)),
            pl.BlockSpec((1, D_MODEL, tf), lambda e, f: (e, 0, f)),
            pl.BlockSpec((1, D_MODEL, tf), lambda e, f: (e, 0, f)),
            pl.BlockSpec((1, tf, D_MODEL), lambda e, f: (e, f, 0)),
        ],
        out_specs=pl.BlockSpec((1, cap, D_MODEL), lambda e, f: (e, 0, 0)),
        out_shape=jax.ShapeDtypeStruct((E, cap, D_MODEL), BF16),
        scratch_shapes=[pltpu.VMEM((cap, D_MODEL), F32)],
        compiler_params=_params("parallel", "arbitrary"),
        name="ffn",
    )(xe, gate, w_g, w_u, w_d)


def _prep_weights(w_in, w_gk_f, w_gk_b, q_norm, k_norm):
    pts = np.cumsum([GLA_QK, GLA_QK, GLA_V, GLA_V, GLA_RANK, GLA_RANK, ATT_Q, ATT_KV, ATT_KV, D_MODEL])
    gq, gk, gv, gg, r_f, r_b, aq, ak, av, z_a, z_b = jnp.split(w_in, pts, axis=-1)
    perm = np.concatenate([np.arange(0, ATT_HD, 2), np.arange(1, ATT_HD, 2)])
    perm_q = (np.arange(ATT_HEADS)[:, None] * ATT_HD + perm[None, :]).reshape(-1)
    perm_k = (np.arange(ATT_KV_HEADS)[:, None] * ATT_HD + perm[None, :]).reshape(-1)
    w_main = jnp.concatenate([gq, gk, gv, gg, z_a, z_b, aq[:, perm_q], ak[:, perm_k], av], axis=-1)
    w_r = jnp.pad(jnp.concatenate([r_f, r_b], axis=-1), ((0, 0), (0, LANE - 2 * GLA_RANK)))
    wf = jnp.pad(w_gk_f, ((0, LANE - GLA_RANK), (0, 0)))
    wb = jnp.pad(w_gk_b, ((GLA_RANK, LANE - 2 * GLA_RANK), (0, 0)))
    return (w_main.astype(BF16), w_r.astype(BF16), wf.astype(BF16), wb.astype(BF16),
            q_norm[perm][None, :], k_norm[perm][None, :])


def _layer(x3, norm_mix, w_main, w_r, wf, bf, wb, bb, gla_norm, qn, kn,
           w_oa, w_ob, w_out, norm_ffn, w_rt, w_e_gate, w_e_up, w_e_down):
    B, S, D = x3.shape
    T = B * S
    x = x3.reshape(T, D)
    proj, r = _inproj(x, norm_mix, w_r, w_main)
    proj3, r3 = proj.reshape(B, S, -1), r.reshape(B, S, LANE)
    o_f = _gla_dir(proj3, r3, wf, bf, True)
    o_a = _gla_dir(proj3, r3, wb, bb, False, o_f, gla_norm)
    cos, sin = _rope_tables(S)
    q, k, v = _qkprep(proj, cos, sin, qn, kn, S)
    o_b = _flash(q.reshape(B, S, ATT_Q), k.reshape(B, S, ATT_KV), v.reshape(B, S, 2 * ATT_KV))
    x1, h2, aff_t = _merge(x, o_a.reshape(T, GLA_V), o_b.reshape(T, ATT_Q), proj,
                           w_oa, w_ob, w_out, norm_ffn, w_rt)
    cap = CAPACITY_FACTOR * T // N_EXPERTS
    slot, offs_gather, offs_scatter = _select(aff_t, cap)
    xe, gate = _gather(offs_gather, slot, aff_t, h2, cap)
    ye = _ffn(xe, gate, w_e_gate, w_e_up, w_e_down)
    out = _scatter(offs_scatter, slot, x1, ye)
    return out.reshape(B, S, D)


def kernel(x_prompt, x_sample, norm_mix, w_in, w_gk_f, b_gk_f, w_gk_b, b_gk_b, gla_norm, q_norm, k_norm,
           w_o_gla, w_o_att, w_out, norm_ffn, w_router, w_e_gate, w_e_up, w_e_down):
    w_main, w_r, wf, wb, qn, kn = _prep_weights(w_in[0], w_gk_f[0], w_gk_b[0], q_norm[0], k_norm[0])
    args = (norm_mix[0][None, :], w_main, w_r, wf, b_gk_f[0][None, :], wb, b_gk_b[0][None, :],
            gla_norm[0][None, :], qn, kn,
            w_o_gla[0].astype(BF16), w_o_att[0].astype(BF16), w_out[0].astype(BF16),
            norm_ffn[0][None, :], w_router[0].T, w_e_gate[0], w_e_up[0], w_e_down[0])
    return (_layer(x_prompt, *args), _layer(x_sample, *args))
```

```python
import functools

import jax
import jax.numpy as jnp
import numpy as np
from jax import lax
from jax.experimental import pallas as pl
from jax.experimental.pallas import tpu as pltpu

D_MODEL = 1024
GRID_W = 64
GLA_HEADS = 4
GLA_DK = 128
GLA_DV = 256
GLA_RANK = 16
GLA_GATE_NORM = 16.0
GLA_CHUNK = 64
GLA_QK = GLA_HEADS * GLA_DK
GLA_V = GLA_HEADS * GLA_DV
ATT_HEADS = 8
ATT_KV_HEADS = 2
ATT_GROUP = ATT_HEADS // ATT_KV_HEADS
ATT_HD = 128
ATT_Q = ATT_HEADS * ATT_HD
ATT_KV = ATT_KV_HEADS * ATT_HD
ROPE_THETA = 10000.0
Q_BLOCK = 128
N_EXPERTS = 16
CAPACITY_FACTOR = 2
EXPERT_FF = 2048
EPS = 1e-6
LOG2_E = 1.4426950408889634

LANE = 128
VMEM_LIMIT = 56 * 1024 * 1024

BF16 = jnp.bfloat16
F32 = jnp.float32
I32 = jnp.int32

COL_GQ, COL_GK, COL_GV, COL_GG = 0, 4, 8, 16
COL_ZA, COL_ZB, COL_AQ, COL_AK, COL_AV = 24, 32, 40, 48, 50
N_MAIN_COLS = 52 * LANE

GLA_BLOCK = 256
GLA_SUB = GLA_BLOCK // GLA_CHUNK
ROUTE_TB = 256
ROUTE_WIN = 128
ROUTE_NWIN = ROUTE_TB // ROUTE_WIN + 1
SCATTER_LEVELS = ((ROUTE_WIN, ROUTE_WIN // 2), (ROUTE_TB, ROUTE_WIN), (ROUTE_NWIN * ROUTE_WIN, ROUTE_WIN))
FLASH_TK = 4096
FLASH_SCORES = 512 * 4096
MIN_NORMAL_F32_BITS = 0x00800000
SELECT_REFINE_STEPS = 32
GATHER_TC = 512
GATHER_WIN = 128
GATHER_SPAN = 3


def _nt_dot(a, b, **kw):
    return lax.dot_general(a, b, (((1,), (1,)), ((), ())), preferred_element_type=F32, **kw)


def _tn_dot(a, b):
    return lax.dot_general(a, b, (((0,), (0,)), ((), ())), preferred_element_type=F32)


def _dot(a, b):
    return jnp.dot(a, b, preferred_element_type=F32)


def _params(*sem):
    return pltpu.CompilerParams(dimension_semantics=sem, vmem_limit_bytes=VMEM_LIMIT)


def _inproj_kernel(x_ref, g_ref, wr_ref, w_ref, o_ref, r_ref, hn_ref):
    @pl.when(pl.program_id(1) == 0)
    def _():
        x = x_ref[...]
        ms = jnp.mean(x * x, axis=-1, keepdims=True)
        hn = (x * lax.rsqrt(ms + EPS) * g_ref[...]).astype(BF16)
        hn_ref[...] = hn
        r_ref[...] = _dot(hn, wr_ref[...])

    o_ref[...] = _dot(hn_ref[...], w_ref[...])


def _inproj(x, g, w_r, w_main, *, tm=1024, n_col_tiles=4):
    T = x.shape[0]
    tn = N_MAIN_COLS // n_col_tiles
    return pl.pallas_call(
        _inproj_kernel,
        grid=(T // tm, n_col_tiles),
        in_specs=[
            pl.BlockSpec((tm, D_MODEL), lambda i, j: (i, 0)),
            pl.BlockSpec((1, D_MODEL), lambda i, j: (0, 0)),
            pl.BlockSpec((D_MODEL, LANE), lambda i, j: (0, 0)),
            pl.BlockSpec((D_MODEL, tn), lambda i, j: (0, j)),
        ],
        out_specs=[
            pl.BlockSpec((tm, tn), lambda i, j: (i, j)),
            pl.BlockSpec((tm, LANE), lambda i, j: (i, 0)),
        ],
        out_shape=[
            jax.ShapeDtypeStruct((T, N_MAIN_COLS), F32),
            jax.ShapeDtypeStruct((T, LANE), F32),
        ],
        scratch_shapes=[pltpu.VMEM((tm, D_MODEL), BF16)],
        compiler_params=_params("parallel", "arbitrary"),
        name="inproj",
    )(x, g, w_r, w_main)


def _log_sigmoid(x):
    return jnp.minimum(x, 0.0) - jnp.log(1.0 + jnp.exp(-jnp.abs(x)))


def _rows(parts):
    return jnp.concatenate([jnp.broadcast_to(p, (GLA_CHUNK, p.shape[-1])) for p in parts], axis=0)


def _gla_masks(fwd):
    n = GLA_BLOCK
    row = lax.broadcasted_iota(I32, (n, n), 0)
    col = lax.broadcasted_iota(I32, (n, n), 1)
    same = (row // GLA_CHUNK) == (col // GLA_CHUNK)
    tri = same & ((col <= row) if fwd else (col >= row))
    dist = (row // GLA_CHUNK - col // GLA_CHUNK) * (1 if fwd else -1)
    return tri, dist


def _gla_prep(q, k, r, w, bias, tri, fwd):
    C = GLA_CHUNK
    lg = _log_sigmoid(_dot(r, w) + bias) * (1.0 / GLA_GATE_NORM)
    hi = lg.astype(BF16)
    rem = lg - hi.astype(F32)
    mid = rem.astype(BF16)
    lo = (rem - mid.astype(F32)).astype(BF16)
    tri_b = jnp.where(tri, 1.0, 0.0).astype(BF16)
    b = _dot(tri_b, hi) + _dot(tri_b, mid) + _dot(tri_b, lo)
    last = (lambda s: s * C + C - 1) if fwd else (lambda s: s * C)
    tot = [b[last(s):last(s) + 1, :] for s in range(GLA_SUB)]
    zero = jnp.zeros_like(tot[0])
    before = [zero, tot[0], tot[0] + tot[1], tot[0] + tot[1] + tot[2]]
    after = [tot[1] + tot[2] + tot[3], tot[2] + tot[3], tot[3], zero]
    lead, trail = (before, after) if fwd else (after, before)

    q_t = q * jnp.exp(b) * (GLA_DK ** -0.5)
    k_t = (k * jnp.exp(-b)).astype(BF16)
    k_hat = k * jnp.exp(_rows(tot) - b)
    q_in = (q_t * _rows([jnp.exp(x) for x in lead])).astype(BF16)
    k_out = (k_hat * _rows([jnp.exp(x) for x in trail])).astype(BF16)
    k_hat = k_hat.astype(BF16)

    e1, e2, e12 = jnp.exp(tot[1]), jnp.exp(tot[2]), jnp.exp(tot[1] + tot[2])
    if fwd:
        q1, q2, q3 = q_t[C:], q_t[2 * C:], q_t[3 * C:]
    else:
        q1, q2, q3 = q_t[:3 * C], q_t[:2 * C], q_t[:C]
    q2 = q2 * jnp.concatenate([jnp.broadcast_to(e1, (C, GLA_DK)), jnp.broadcast_to(e2, (C, GLA_DK))], axis=0)
    q3 = q3 * e12
    q_far = jnp.concatenate([q1, q2, q3], axis=0).astype(BF16)
    decay = jnp.exp(tot[0] + tot[1] + tot[2] + tot[3])
    return q_t.astype(BF16), k_t, q_far, k_hat, q_in, k_out, decay


def _gla_finish(ops, v, st, tri, dist, fwd):
    C = GLA_CHUNK
    q_t, k_t, q_far, k_hat, q_in, k_out, decay = ops
    y = _nt_dot(q_far, k_hat)
    y1, y2, y3 = y[:3 * C], y[3 * C:5 * C], y[5 * C:]
    z = lambda rows: jnp.zeros((rows, GLA_BLOCK), F32)
    if fwd:
        y1, y2, y3 = (jnp.concatenate([z(C), y1], 0), jnp.concatenate([z(2 * C), y2], 0),
                      jnp.concatenate([z(3 * C), y3], 0))
    else:
        y1, y2, y3 = (jnp.concatenate([y1, z(C)], 0), jnp.concatenate([y2, z(2 * C)], 0),
                      jnp.concatenate([y3, z(3 * C)], 0))
    x0 = _nt_dot(q_t, k_t)
    a = jnp.where(tri, x0, jnp.where(dist == 1, y1, jnp.where(dist == 2, y2, jnp.where(dist == 3, y3, 0.0))))
    o = _dot(a.astype(BF16), v) + _nt_dot(q_in, st.astype(BF16))
    st_new = st * decay + _tn_dot(v, k_out)
    return o, st_new


def _gla_dir_kernel(*refs, L, fwd):
    if fwd:
        q_ref, k_ref, v_ref, r_ref, w_ref, b_ref, o_ref, st_ref = refs
    else:
        q_ref, k_ref, v_ref, r_ref, w_ref, b_ref, g_ref, of_ref, gn_ref, o_ref, st_ref = refs
    nb = L // GLA_BLOCK

    @pl.when(pl.program_id(1) == 0)
    def _():
        st_ref[...] = jnp.zeros_like(st_ref)

    tri, dist = _gla_masks(fwd)

    def body(i, carry):
        blk = i if fwd else nb - 1 - i
        sl = pl.ds(pl.multiple_of(blk * GLA_BLOCK, GLA_BLOCK), GLA_BLOCK)
        r = r_ref[0, sl, :].astype(BF16)

        def prep(h):
            ks = slice(h * GLA_DK, (h + 1) * GLA_DK)
            return _gla_prep(q_ref[0, sl, ks], k_ref[0, sl, ks], r, w_ref[:, ks], b_ref[:, ks], tri, fwd)

        nxt = prep(0)
        for h in range(GLA_HEADS):
            vs = slice(h * GLA_DV, (h + 1) * GLA_DV)
            ops = nxt
            if h + 1 < GLA_HEADS:
                nxt = prep(h + 1)
            o, st = _gla_finish(ops, v_ref[0, sl, vs].astype(BF16), st_ref[h], tri, dist, fwd)
            st_ref[h] = st
            if fwd:
                o_ref[0, sl, vs] = o
            else:
                o = o + of_ref[0, sl, vs]
                ms = jnp.mean(o * o, axis=-1, keepdims=True)
                o = o * lax.rsqrt(ms + EPS) * gn_ref[...]
                g = g_ref[0, sl, vs]
                o_ref[0, sl, vs] = (o * (g * jax.nn.sigmoid(g))).astype(o_ref.dtype)
        return carry

    lax.fori_loop(0, nb, body, 0)


def _gla_dir(proj3, r3, w, bias, fwd, o_f=None, gn=None, *, L=512):
    B, S, _ = proj3.shape
    nl = S // L
    pos = (lambda n: n) if fwd else (lambda n: nl - 1 - n)
    full = lambda b, n: (0, 0)
    in_specs = [
        pl.BlockSpec((1, L, GLA_QK), lambda b, n: (b, pos(n), COL_GQ * LANE // GLA_QK)),
        pl.BlockSpec((1, L, GLA_QK), lambda b, n: (b, pos(n), COL_GK * LANE // GLA_QK)),
        pl.BlockSpec((1, L, GLA_V), lambda b, n: (b, pos(n), COL_GV * LANE // GLA_V)),
        pl.BlockSpec((1, L, LANE), lambda b, n: (b, pos(n), 0)),
        pl.BlockSpec((LANE, GLA_QK), full),
        pl.BlockSpec((1, GLA_QK), full),
    ]
    args = [proj3, proj3, proj3, r3, w, bias]
    if not fwd:
        in_specs += [
            pl.BlockSpec((1, L, GLA_V), lambda b, n: (b, pos(n), COL_GG * LANE // GLA_V)),
            pl.BlockSpec((1, L, GLA_V), lambda b, n: (b, pos(n), 0)),
            pl.BlockSpec((1, GLA_DV), full),
        ]
        args += [proj3, o_f, gn]
    return pl.pallas_call(
        functools.partial(_gla_dir_kernel, L=L, fwd=fwd),
        grid=(B, nl),
        in_specs=in_specs,
        out_specs=pl.BlockSpec((1, L, GLA_V), lambda b, n: (b, pos(n), 0)),
        out_shape=jax.ShapeDtypeStruct((B, S, GLA_V), F32 if fwd else BF16),
        scratch_shapes=[pltpu.VMEM((GLA_HEADS, GLA_DV, GLA_DK), F32)],
        compiler_params=_params("parallel", "arbitrary"),
        name="gla_fwd" if fwd else "gla_bwd",
    )(*args)


def _rope_tables(S):
    rows = S // GRID_W
    pos_r = np.repeat(np.arange(rows), GRID_W).astype(np.float32)
    pos_c = np.tile(np.arange(GRID_W), rows).astype(np.float32)
    half = ATT_HD // 2
    inv = jnp.asarray(ROPE_THETA, F32) ** (-jnp.arange(0, half, 2, dtype=F32) / half)
    ang = jnp.concatenate([pos_r[:, None] * inv, pos_c[:, None] * inv], axis=-1)
    cos, sin = jnp.cos(ang), jnp.sin(ang)
    return jnp.concatenate([cos, cos], axis=-1), jnp.concatenate([-sin, sin], axis=-1)


def _qkprep_kernel(aq_ref, akv_ref, cos_ref, sin_ref, qn_ref, kn_ref, q_out, k_out, v_out):
    cos = cos_ref[...]
    sin = sin_ref[...]

    def norm_rope(x, w, scale):
        ms = jnp.mean(x * x, axis=-1, keepdims=True)
        y = x * lax.rsqrt(ms + EPS) * w
        return (y * cos + pltpu.roll(y, ATT_HD // 2, 1) * sin) * scale

    for h in range(ATT_HEADS):
        sl = slice(h * ATT_HD, (h + 1) * ATT_HD)
        q_out[:, sl] = norm_rope(aq_ref[:, sl], qn_ref[...], ATT_HD ** -0.5 * LOG2_E).astype(BF16)
    for h in range(ATT_KV_HEADS):
        sl = slice(h * ATT_HD, (h + 1) * ATT_HD)
        k_out[:, sl] = norm_rope(akv_ref[:, sl], kn_ref[...], 1.0).astype(BF16)
    for h in range(ATT_KV_HEADS):
        v_out[:, 2 * h * ATT_HD:(2 * h + 1) * ATT_HD] = akv_ref[:, ATT_KV + h * ATT_HD:ATT_KV + (h + 1) * ATT_HD].astype(BF16)
        v_out[:, (2 * h + 1) * ATT_HD:(2 * h + 2) * ATT_HD] = jnp.ones((v_out.shape[0], ATT_HD), BF16)


def _qkprep(proj, cos, sin, qn, kn, S, *, tm=512):
    T = proj.shape[0]
    ns = S // tm
    return pl.pallas_call(
        _qkprep_kernel,
        grid=(T // tm,),
        in_specs=[
            pl.BlockSpec((tm, ATT_Q), lambda i: (i, COL_AQ * LANE // ATT_Q)),
            pl.BlockSpec((tm, 2 * ATT_KV), lambda i: (i, COL_AK * LANE // (2 * ATT_KV))),
            pl.BlockSpec((tm, ATT_HD), lambda i: (i % ns, 0)),
            pl.BlockSpec((tm, ATT_HD), lambda i: (i % ns, 0)),
            pl.BlockSpec((1, ATT_HD), lambda i: (0, 0)),
            pl.BlockSpec((1, ATT_HD), lambda i: (0, 0)),
        ],
        out_specs=[
            pl.BlockSpec((tm, ATT_Q), lambda i: (i, 0)),
            pl.BlockSpec((tm, ATT_KV), lambda i: (i, 0)),
            pl.BlockSpec((tm, 2 * ATT_KV), lambda i: (i, 0)),
        ],
        out_shape=[
            jax.ShapeDtypeStruct((T, ATT_Q), BF16),
            jax.ShapeDtypeStruct((T, ATT_KV), BF16),
            jax.ShapeDtypeStruct((T, 2 * ATT_KV), BF16),
        ],
        compiler_params=_params("parallel"),
        name="qkprep",
    )(proj, proj, cos, sin, qn, kn)


def _flash_kernel(q_ref, k_ref, v_ref, o_ref, m_ref, acc_ref, *, tq):
    ki = pl.program_id(3)
    tk = k_ref.shape[1]

    @pl.when(ki == 0)
    def _():
        m_ref[...] = jnp.full_like(m_ref, -jnp.inf)
        acc_ref[...] = jnp.zeros_like(acc_ref)

    k = k_ref[0]
    v = v_ref[0]

    def scores(g):
        return _nt_dot(q_ref[0, :, g * ATT_HD:(g + 1) * ATT_HD], k)

    def update(g, s):
        part = s[:, :LANE]
        for j in range(1, tk // LANE):
            part = jnp.maximum(part, s[:, j * LANE:(j + 1) * LANE])
        m = m_ref[g]
        m_new = jnp.maximum(m, jnp.max(part, axis=-1, keepdims=True))
        p = jnp.exp2(s - m_new).astype(BF16)
        acc_ref[g] = jnp.exp2(m - m_new) * acc_ref[g] + _dot(p, v)
        m_ref[g] = m_new

    s_next = scores(0)
    for g in range(ATT_GROUP):
        s = s_next
        if g + 1 < ATT_GROUP:
            s_next = scores(g + 1)
        update(g, s)

    @pl.when(ki == pl.num_programs(3) - 1)
    def _():
        for g in range(ATT_GROUP):
            acc = acc_ref[g]
            out = acc[:, :ATT_HD] / acc[:, ATT_HD:ATT_HD + 1]
            for u in range(tq // Q_BLOCK):
                rows = slice(u * Q_BLOCK, (u + 1) * Q_BLOCK)
                o_ref[0, g, :, u * ATT_HD:(u + 1) * ATT_HD] = out[rows, :].astype(o_ref.dtype)


def _flash(q3, k3, v3):
    B, S, _ = q3.shape
    gw = ATT_GROUP * ATT_HD
    tk = min(FLASH_TK, S)
    tq = min(FLASH_SCORES // tk, S)
    n_qt = S // tq
    return pl.pallas_call(
        functools.partial(_flash_kernel, tq=tq),
        grid=(B, ATT_KV_HEADS, S // tq, S // tk),
        in_specs=[
            pl.BlockSpec((1, tq, gw), lambda b, h, qi, ki: (b, qi, h)),
            pl.BlockSpec((1, tk, ATT_HD), lambda b, h, qi, ki: (b, ki, h)),
            pl.BlockSpec((1, tk, 2 * ATT_HD), lambda b, h, qi, ki: (b, ki, h)),
        ],
        out_specs=pl.BlockSpec((1, ATT_GROUP, Q_BLOCK, tq), lambda b, h, qi, ki: (b, 0, 0, h * n_qt + qi)),
        out_shape=jax.ShapeDtypeStruct((B, ATT_GROUP, Q_BLOCK, ATT_KV_HEADS * S), BF16),
        scratch_shapes=[
            pltpu.VMEM((ATT_GROUP, tq, 1), F32),
            pltpu.VMEM((ATT_GROUP, tq, 2 * ATT_HD), F32),
        ],
        compiler_params=_params("parallel", "parallel", "parallel", "arbitrary"),
        name="flash",
    )(q3, k3, v3)


def _merge_kernel(x_ref, oa_ref, ob_ref, za_ref, zb_ref, woa_ref, wob_ref, wout_ref, nf_ref, wr_ref,
                  x1_ref, h2_ref, aff_ref):
    ya = _dot(oa_ref[...], woa_ref[...])
    yb = _dot(ob_ref[...], wob_ref[...])
    merged = jax.nn.sigmoid(za_ref[...]) * ya + jax.nn.sigmoid(zb_ref[...]) * yb
    x1 = x_ref[...] + _dot(merged.astype(BF16), wout_ref[...])
    x1_ref[...] = x1
    ms = jnp.mean(x1 * x1, axis=-1, keepdims=True)
    h2 = x1 * lax.rsqrt(ms + EPS) * nf_ref[...]
    h2_ref[...] = h2.astype(BF16)
    logits = _nt_dot(wr_ref[...], h2, precision=lax.Precision.HIGHEST)
    e = jnp.exp(logits - jnp.max(logits, axis=0, keepdims=True))
    aff_ref[...] = e / jnp.sum(e, axis=0, keepdims=True)


def _merge(x, o_a, o_b, proj, w_oa, w_ob, w_out, nf, w_rt, *, tm=512):
    T = x.shape[0]
    full = lambda i: (0, 0)
    return pl.pallas_call(
        _merge_kernel,
        grid=(T // tm,),
        in_specs=[
            pl.BlockSpec((tm, D_MODEL), lambda i: (i, 0)),
            pl.BlockSpec((tm, GLA_V), lambda i: (i, 0)),
            pl.BlockSpec((tm, ATT_Q), lambda i: (i, 0)),
            pl.BlockSpec((tm, D_MODEL), lambda i: (i, COL_ZA * LANE // D_MODEL)),
            pl.BlockSpec((tm, D_MODEL), lambda i: (i, COL_ZB * LANE // D_MODEL)),
            pl.BlockSpec((GLA_V, D_MODEL), full),
            pl.BlockSpec((ATT_Q, D_MODEL), full),
            pl.BlockSpec((D_MODEL, D_MODEL), full),
            pl.BlockSpec((1, D_MODEL), full),
            pl.BlockSpec((N_EXPERTS, D_MODEL), full),
        ],
        out_specs=[
            pl.BlockSpec((tm, D_MODEL), lambda i: (i, 0)),
            pl.BlockSpec((tm, D_MODEL), lambda i: (i, 0)),
            pl.BlockSpec((N_EXPERTS, tm), lambda i: (0, i)),
        ],
        out_shape=[
            jax.ShapeDtypeStruct((T, D_MODEL), F32),
            jax.ShapeDtypeStruct((T, D_MODEL), BF16),
            jax.ShapeDtypeStruct((N_EXPERTS, T), F32),
        ],
        compiler_params=_params("parallel"),
        name="merge",
    )(x, o_a, o_b, proj, proj, w_oa, w_ob, w_out, nf, w_rt)


def _lane_cumsum(x):
    n = x.shape[1]
    lane = lax.broadcasted_iota(I32, x.shape, 1)
    shift = 1
    while shift < n:
        x = x + jnp.where(lane >= shift, pltpu.roll(x, shift, 1), 0)
        shift *= 2
    return x


def _select_kernel(aff_ref, slot_ref, incl_ref, *, cap):
    aff = aff_ref[...]

    def at_least_cap(value):
        return jnp.sum((aff >= value).astype(I32), axis=1, keepdims=True) >= cap

    def body(i, thr):
        cand = thr | jnp.left_shift(jnp.int32(1), 30 - i)
        return jnp.where(at_least_cap(pltpu.bitcast(cand, F32)), cand, thr)

    thr = lax.fori_loop(0, 31, body, jnp.zeros((aff.shape[0], 1), I32))

    lo = pltpu.bitcast(thr, F32).astype(aff.dtype)
    hi = pltpu.bitcast(jnp.maximum(thr + 1, MIN_NORMAL_F32_BITS), F32).astype(aff.dtype)

    def refine(_, bracket):
        lo, hi = bracket
        mid = 0.5 * (lo + hi)
        enough = at_least_cap(mid)
        return jnp.where(enough, mid, lo), jnp.where(enough, hi, mid)

    lo, hi = lax.fori_loop(0, SELECT_REFINE_STEPS, refine, (lo, hi))
    above = aff >= hi
    need = cap - jnp.sum(above.astype(I32), axis=1, keepdims=True)
    ties = (aff >= lo) & (aff < hi)
    sel = above | (ties & (_lane_cumsum(ties.astype(I32)) <= need))
    incl = _lane_cumsum(sel.astype(I32))
    slot_ref[...] = jnp.where(sel, incl - 1, -1)
    incl_ref[...] = incl


def _select(aff_t, cap):
    E, T = aff_t.shape
    full = lambda: (0, 0)
    slot, incl = pl.pallas_call(
        functools.partial(_select_kernel, cap=cap),
        in_specs=[pl.BlockSpec((E, T), full)],
        out_specs=[pl.BlockSpec((E, T), full), pl.BlockSpec((E, T), full)],
        out_shape=[jax.ShapeDtypeStruct((E, T), I32), jax.ShapeDtypeStruct((E, T), I32)],
        compiler_params=pltpu.CompilerParams(vmem_limit_bytes=VMEM_LIMIT),
        name="select",
    )(aff_t)
    def before(step):
        return jnp.concatenate([jnp.zeros((E, 1), I32), incl[:, step - 1::step]], axis=1)

    return slot, before(GATHER_TC), before(ROUTE_TB)


def _gather_kernel(offs_ref, slot_ref, gate_ref, h_ref, xe_ref, gc_ref, acc_ref, g_ref, ptr_ref):
    e = pl.program_id(0)
    base = pl.program_id(1) * GATHER_WIN

    first = lax.while_loop(lambda c: offs_ref[e, c + 1] <= base, lambda c: c + 1,
                           jnp.where(base == 0, 0, ptr_ref[0]))
    ptr_ref[0] = first
    stop = lax.while_loop(lambda c: offs_ref[e, c] < base + GATHER_WIN, lambda c: c + 1, first)

    def picked(slots, gates, rows):
        want = lax.broadcasted_iota(I32, (GATHER_WIN, slots.shape[1]), 0) + base
        hit = slots == want
        return (_dot(jnp.where(hit, 1.0, 0.0).astype(BF16), h_ref[rows, :]),
                jnp.sum(jnp.where(hit, gates, 0.0), axis=1, keepdims=True))

    c0 = jnp.minimum(first, slot_ref.shape[1] - GATHER_SPAN)
    span = lambda ref: jnp.concatenate([ref[0, pl.ds(c0 + j, 1), :] for j in range(GATHER_SPAN)], axis=1)
    acc_ref[...], g_ref[...] = picked(span(slot_ref), span(gate_ref),
                                      pl.ds(pl.multiple_of(c0 * GATHER_TC, GATHER_TC), GATHER_SPAN * GATHER_TC))

    def body(c, carry):
        acc, g = picked(slot_ref[0, pl.ds(c, 1), :], gate_ref[0, pl.ds(c, 1), :],
                        pl.ds(pl.multiple_of(c * GATHER_TC, GATHER_TC), GATHER_TC))
        acc_ref[...] += acc
        g_ref[...] += g
        return carry

    lax.fori_loop(c0 + GATHER_SPAN, stop, body, 0)
    xe_ref[0] = acc_ref[...].astype(xe_ref.dtype)
    gc_ref[0] = g_ref[...]


def _gather(offs, slot, aff_t, h2, cap):
    E, T = slot.shape
    n_chunks = T // GATHER_TC
    assert cap % GATHER_WIN == 0 and T % GATHER_TC == 0 and n_chunks >= GATHER_SPAN
    row_spec = pl.BlockSpec((1, n_chunks, GATHER_TC), lambda e, w, offs: (e, 0, 0))
    return pl.pallas_call(
        _gather_kernel,
        grid_spec=pltpu.PrefetchScalarGridSpec(
            num_scalar_prefetch=1,
            grid=(E, cap // GATHER_WIN),
            in_specs=[row_spec, row_spec,
                      pl.BlockSpec((T, D_MODEL), lambda e, w, offs: (0, 0), pipeline_mode=pl.Buffered(1))],
            out_specs=[
                pl.BlockSpec((1, GATHER_WIN, D_MODEL), lambda e, w, offs: (e, w, 0)),
                pl.BlockSpec((1, GATHER_WIN, 1), lambda e, w, offs: (e, w, 0)),
            ],
            scratch_shapes=[pltpu.VMEM((GATHER_WIN, D_MODEL), F32), pltpu.VMEM((GATHER_WIN, 1), F32),
                            pltpu.SMEM((1,), I32)],
        ),
        out_shape=[
            jax.ShapeDtypeStruct((E, cap, D_MODEL), BF16),
            jax.ShapeDtypeStruct((E, cap, 1), F32),
        ],
        compiler_params=_params("arbitrary", "arbitrary"),
        name="gather",
    )(offs, slot.reshape(E, n_chunks, GATHER_TC), aff_t.reshape(E, n_chunks, GATHER_TC), h2)


def _scatter_window(offs_ref, e, t, cap, level):
    rows, align = level
    return pl.multiple_of(jnp.minimum((offs_ref[e, t] // align) * align, cap - rows), align)


def _scatter_copy(offs_ref, y_hbm, ybuf, sem, e, t, buf, cap, level):
    start = _scatter_window(offs_ref, e, t, cap, level)
    rows = level[0]
    return pltpu.make_async_copy(y_hbm.at[e, pl.ds(start, rows), :],
                                 ybuf.at[buf, pl.ds(e * rows, rows), :], sem.at[buf])


def _scatter_fits(offs_ref, t, level):
    rows, align = level
    fits = True
    for e in range(N_EXPERTS):
        off = offs_ref[e, t]
        fits = fits & ((off % align) + (offs_ref[e, t + 1] - off) <= rows)
    return fits


def _scatter_kernel(offs_ref, slot_ref, x1_ref, y_hbm, o_ref, ybuf, sem, *, cap, levels):
    t = pl.program_id(0)
    buf = t % 2

    def by_level(tt, fn):
        taken = False
        for level in levels[:-1]:
            use = jnp.logical_not(taken) & _scatter_fits(offs_ref, tt, level)
            pl.when(use)(functools.partial(fn, level))
            taken = taken | use
        pl.when(jnp.logical_not(taken))(functools.partial(fn, levels[-1]))

    def start(tt, b, level):
        for e in range(N_EXPERTS):
            _scatter_copy(offs_ref, y_hbm, ybuf, sem, e, tt, b, cap, level).start()

    @pl.when(t == 0)
    def _():
        by_level(t, functools.partial(start, t, buf))

    @pl.when(t + 1 < pl.num_programs(0))
    def _():
        by_level(t + 1, functools.partial(start, t + 1, 1 - buf))

    def finish(level):
        rows = level[0]
        lane = lax.broadcasted_iota(I32, (ROUTE_TB, rows), 1)
        hits = []
        for e in range(N_EXPERTS):
            rel = slot_ref[:, e:e + 1] - _scatter_window(offs_ref, e, t, cap, level)
            hits.append(jnp.where(rel == lane, 1.0, 0.0).astype(BF16))
        onehot = jnp.concatenate(hits, axis=1)
        for e in range(N_EXPERTS):
            _scatter_copy(offs_ref, y_hbm, ybuf, sem, e, t, buf, cap, level).wait()
        o_ref[...] = x1_ref[...] + _dot(onehot, ybuf[buf, pl.ds(0, N_EXPERTS * rows), :])

    by_level(t, finish)


def _scatter(offs, slot, x1, ye):
    E, T = slot.shape
    cap = ye.shape[1]
    levels = tuple((min(r, cap), a) for r, a in SCATTER_LEVELS)
    rows = levels[-1][0]
    return pl.pallas_call(
        functools.partial(_scatter_kernel, cap=cap, levels=levels),
        grid_spec=pltpu.PrefetchScalarGridSpec(
            num_scalar_prefetch=1,
            grid=(T // ROUTE_TB,),
            in_specs=[
                pl.BlockSpec((ROUTE_TB, E), lambda t, offs: (t, 0)),
                pl.BlockSpec((ROUTE_TB, D_MODEL), lambda t, offs: (t, 0)),
                pl.BlockSpec(memory_space=pl.ANY),
            ],
            out_specs=pl.BlockSpec((ROUTE_TB, D_MODEL), lambda t, offs: (t, 0)),
            scratch_shapes=[pltpu.VMEM((2, E * rows, D_MODEL), BF16), pltpu.SemaphoreType.DMA((2,))],
        ),
        out_shape=jax.ShapeDtypeStruct((T, D_MODEL), F32),
        compiler_params=_params("arbitrary"),
        name="scatter",
    )(offs, slot.T, x1, ye)


def _ffn_kernel(xe_ref, gate_ref, wg_ref, wu_ref, wd_ref, o_ref, acc_ref, *, rows):
    f = pl.program_id(1)
    cap = xe_ref.shape[1]

    @pl.when(f == 0)
    def _():
        acc_ref[...] = jnp.zeros_like(acc_ref)

    wg = wg_ref[0].astype(BF16)
    wu = wu_ref[0].astype(BF16)

    def up(i):
        xe = xe_ref[0, i * rows:(i + 1) * rows, :]
        return _dot(xe, wg), _dot(xe, wu)

    nxt = up(0)
    wd = wd_ref[0].astype(BF16)
    for i in range(cap // rows):
        hg, hu = nxt
        if i + 1 < cap // rows:
            nxt = up(i + 1)
        hid = (hg * jax.nn.sigmoid(hg) * hu).astype(BF16)
        acc_ref[i * rows:(i + 1) * rows, :] += _dot(hid, wd)

    @pl.when(f == pl.num_programs(1) - 1)
    def _():
        o_ref[0] = (acc_ref[...] * gate_ref[0]).astype(o_ref.dtype)


def _ffn(xe, gate, w_g, w_u, w_d, *, tf=512, rows=1024):
    E, cap, _ = xe.shape
    rows = min(rows, cap)
    return pl.pallas_call(
        functools.partial(_ffn_kernel, rows=rows),
        grid=(E, EXPERT_FF // tf),
        in_specs=[
            pl.BlockSpec((1, cap, D_MODEL), lambda e, f: (e, 0, 0)),
            pl.BlockSpec((1, cap, 1), lambda e, f: (e, 0, 0)),
            pl.BlockSpec((1, D_MODEL, tf), lambda e, f: (e, 0, f)),
            pl.BlockSpec((1, D_MODEL, tf), lambda e, f: (e, 0, f)),
            pl.BlockSpec((1, tf, D_MODEL), lambda e, f: (e, f, 0)),
        ],
        out_specs=pl.BlockSpec((1, cap, D_MODEL), lambda e, f: (e, 0, 0)),
        out_shape=jax.ShapeDtypeStruct((E, cap, D_MODEL), BF16),
        scratch_shapes=[pltpu.VMEM((cap, D_MODEL), F32)],
        compiler_params=_params("parallel", "arbitrary"),
        name="ffn",
    )(xe, gate, w_g, w_u, w_d)


def _prep_weights(w_in, w_gk_f, w_gk_b, q_norm, k_norm):
    pts = np.cumsum([GLA_QK, GLA_QK, GLA_V, GLA_V, GLA_RANK, GLA_RANK, ATT_Q, ATT_KV, ATT_KV, D_MODEL])
    gq, gk, gv, gg, r_f, r_b, aq, ak, av, z_a, z_b = jnp.split(w_in, pts, axis=-1)
    perm = np.concatenate([np.arange(0, ATT_HD, 2), np.arange(1, ATT_HD, 2)])
    perm_q = (np.arange(ATT_HEADS)[:, None] * ATT_HD + perm[None, :]).reshape(-1)
    perm_k = (np.arange(ATT_KV_HEADS)[:, None] * ATT_HD + perm[None, :]).reshape(-1)
    w_main = jnp.concatenate([gq, gk, gv, gg, z_a, z_b, aq[:, perm_q], ak[:, perm_k], av], axis=-1)
    w_r = jnp.pad(jnp.concatenate([r_f, r_b], axis=-1), ((0, 0), (0, LANE - 2 * GLA_RANK)))
    wf = jnp.pad(w_gk_f, ((0, LANE - GLA_RANK), (0, 0)))
    wb = jnp.pad(w_gk_b, ((GLA_RANK, LANE - 2 * GLA_RANK), (0, 0)))
    return (w_main.astype(BF16), w_r.astype(BF16), wf.astype(BF16), wb.astype(BF16),
            q_norm[perm][None, :], k_norm[perm][None, :])


def _layer(x3, norm_mix, w_main, w_r, wf, bf, wb, bb, gla_norm, qn, kn,
           w_oa, w_ob, w_out, norm_ffn, w_rt, w_e_gate, w_e_up, w_e_down):
    B, S, D = x3.shape
    T = B * S
    x = x3.reshape(T, D)
    proj, r = _inproj(x, norm_mix, w_r, w_main)
    proj3, r3 = proj.reshape(B, S, -1), r.reshape(B, S, LANE)
    o_f = _gla_dir(proj3, r3, wf, bf, True)
    o_a = _gla_dir(proj3, r3, wb, bb, False, o_f, gla_norm)
    cos, sin = _rope_tables(S)
    q, k, v = _qkprep(proj, cos, sin, qn, kn, S)
    o_b = _flash(q.reshape(B, S, ATT_Q), k.reshape(B, S, ATT_KV), v.reshape(B, S, 2 * ATT_KV))
    x1, h2, aff_t = _merge(x, o_a.reshape(T, GLA_V), o_b.reshape(T, ATT_Q), proj,
                           w_oa, w_ob, w_out, norm_ffn, w_rt)
    cap = CAPACITY_FACTOR * T // N_EXPERTS
    slot, offs_gather, offs_scatter = _select(aff_t, cap)
    xe, gate = _gather(offs_gather, slot, aff_t, h2, cap)
    ye = _ffn(xe, gate, w_e_gate, w_e_up, w_e_down)
    out = _scatter(offs_scatter, slot, x1, ye)
    return out.reshape(B, S, D)


def kernel(x_prompt, x_sample, norm_mix, w_in, w_gk_f, b_gk_f, w_gk_b, b_gk_b, gla_norm, q_norm, k_norm,
           w_o_gla, w_o_att, w_out, norm_ffn, w_router, w_e_gate, w_e_up, w_e_down):
    w_main, w_r, wf, wb, qn, kn = _prep_weights(w_in[0], w_gk_f[0], w_gk_b[0], q_norm[0], k_norm[0])
    args = (norm_mix[0][None, :], w_main, w_r, wf, b_gk_f[0][None, :], wb, b_gk_b[0][None, :],
            gla_norm[0][None, :], qn, kn,
            w_o_gla[0].astype(BF16), w_o_att[0].astype(BF16), w_out[0].astype(BF16),
            norm_ffn[0][None, :], w_router[0].T, w_e_gate[0], w_e_up[0], w_e_down[0])
    return (_layer(x_prompt, *args), _layer(x_sample, *args))
```

```python
import functools

import jax
import jax.numpy as jnp
import numpy as np
from jax import lax
from jax.experimental import pallas as pl
from jax.experimental.pallas import tpu as pltpu

D_MODEL = 1024
GRID_W = 64
GLA_HEADS = 4
GLA_DK = 128
GLA_DV = 256
GLA_RANK = 16
GLA_GATE_NORM = 16.0
GLA_CHUNK = 64
GLA_QK = GLA_HEADS * GLA_DK
GLA_V = GLA_HEADS * GLA_DV
ATT_HEADS = 8
ATT_KV_HEADS = 2
ATT_GROUP = ATT_HEADS // ATT_KV_HEADS
ATT_HD = 128
ATT_Q = ATT_HEADS * ATT_HD
ATT_KV = ATT_KV_HEADS * ATT_HD
ROPE_THETA = 10000.0
Q_BLOCK = 128
N_EXPERTS = 16
CAPACITY_FACTOR = 2
EXPERT_FF = 2048
EPS = 1e-6
LOG2_E = 1.4426950408889634

LANE = 128
VMEM_LIMIT = 56 * 1024 * 1024

BF16 = jnp.bfloat16
F32 = jnp.float32
I32 = jnp.int32

COL_GQ, COL_GK, COL_GV, COL_GG = 0, 4, 8, 16
COL_ZA, COL_ZB, COL_AQ, COL_AK, COL_AV = 24, 32, 40, 48, 50
N_MAIN_COLS = 52 * LANE

GLA_BLOCK = 256
GLA_SUB = GLA_BLOCK // GLA_CHUNK
ROUTE_TB = 256
ROUTE_WIN = 128
ROUTE_NWIN = ROUTE_TB // ROUTE_WIN + 1
SCATTER_LEVELS = ((ROUTE_WIN, ROUTE_WIN // 2), (ROUTE_TB, ROUTE_WIN), (ROUTE_NWIN * ROUTE_WIN, ROUTE_WIN))
FLASH_TK = 4096
FLASH_SCORES = 512 * 4096
MIN_NORMAL_F32_BITS = 0x00800000
SELECT_REFINE_STEPS = 32
GATHER_TC = 256
GATHER_WIN = 128
GATHER_SPAN = 5


def _nt_dot(a, b, **kw):
    return lax.dot_general(a, b, (((1,), (1,)), ((), ())), preferred_element_type=F32, **kw)


def _tn_dot(a, b):
    return lax.dot_general(a, b, (((0,), (0,)), ((), ())), preferred_element_type=F32)


def _dot(a, b):
    return jnp.dot(a, b, preferred_element_type=F32)


def _params(*sem):
    return pltpu.CompilerParams(dimension_semantics=sem, vmem_limit_bytes=VMEM_LIMIT)


def _inproj_kernel(x_ref, g_ref, wr_ref, w_ref, o_ref, r_ref, hn_ref):
    @pl.when(pl.program_id(1) == 0)
    def _():
        x = x_ref[...]
        ms = jnp.mean(x * x, axis=-1, keepdims=True)
        hn = (x * lax.rsqrt(ms + EPS) * g_ref[...]).astype(BF16)
        hn_ref[...] = hn
        r_ref[...] = _dot(hn, wr_ref[...])

    o_ref[...] = _dot(hn_ref[...], w_ref[...])


def _inproj(x, g, w_r, w_main, *, tm=1024, n_col_tiles=4):
    T = x.shape[0]
    tn = N_MAIN_COLS // n_col_tiles
    return pl.pallas_call(
        _inproj_kernel,
        grid=(T // tm, n_col_tiles),
        in_specs=[
            pl.BlockSpec((tm, D_MODEL), lambda i, j: (i, 0)),
            pl.BlockSpec((1, D_MODEL), lambda i, j: (0, 0)),
            pl.BlockSpec((D_MODEL, LANE), lambda i, j: (0, 0)),
            pl.BlockSpec((D_MODEL, tn), lambda i, j: (0, j)),
        ],
        out_specs=[
            pl.BlockSpec((tm, tn), lambda i, j: (i, j)),
            pl.BlockSpec((tm, LANE), lambda i, j: (i, 0)),
        ],
        out_shape=[
            jax.ShapeDtypeStruct((T, N_MAIN_COLS), F32),
            jax.ShapeDtypeStruct((T, LANE), F32),
        ],
        scratch_shapes=[pltpu.VMEM((tm, D_MODEL), BF16)],
        compiler_params=_params("parallel", "arbitrary"),
        name="inproj",
    )(x, g, w_r, w_main)


def _log_sigmoid(x):
    return jnp.minimum(x, 0.0) - jnp.log(1.0 + jnp.exp(-jnp.abs(x)))


def _rows(parts):
    return jnp.concatenate([jnp.broadcast_to(p, (GLA_CHUNK, p.shape[-1])) for p in parts], axis=0)


def _gla_masks(fwd):
    n = GLA_BLOCK
    row = lax.broadcasted_iota(I32, (n, n), 0)
    col = lax.broadcasted_iota(I32, (n, n), 1)
    same = (row // GLA_CHUNK) == (col // GLA_CHUNK)
    tri = same & ((col <= row) if fwd else (col >= row))
    dist = (row // GLA_CHUNK - col // GLA_CHUNK) * (1 if fwd else -1)
    return tri, dist


def _gla_prep(q, k, r, w, bias, tri, fwd):
    C = GLA_CHUNK
    lg = _log_sigmoid(_dot(r, w) + bias) * (1.0 / GLA_GATE_NORM)
    hi = lg.astype(BF16)
    rem = lg - hi.astype(F32)
    mid = rem.astype(BF16)
    lo = (rem - mid.astype(F32)).astype(BF16)
    tri_b = jnp.where(tri, 1.0, 0.0).astype(BF16)
    b = _dot(tri_b, hi) + _dot(tri_b, mid) + _dot(tri_b, lo)
    last = (lambda s: s * C + C - 1) if fwd else (lambda s: s * C)
    tot = [b[last(s):last(s) + 1, :] for s in range(GLA_SUB)]
    zero = jnp.zeros_like(tot[0])
    before = [zero, tot[0], tot[0] + tot[1], tot[0] + tot[1] + tot[2]]
    after = [tot[1] + tot[2] + tot[3], tot[2] + tot[3], tot[3], zero]
    lead, trail = (before, after) if fwd else (after, before)

    q_t = q * jnp.exp(b) * (GLA_DK ** -0.5)
    k_t = (k * jnp.exp(-b)).astype(BF16)
    k_hat = k * jnp.exp(_rows(tot) - b)
    q_in = (q_t * _rows([jnp.exp(x) for x in lead])).astype(BF16)
    k_out = (k_hat * _rows([jnp.exp(x) for x in trail])).astype(BF16)
    k_hat = k_hat.astype(BF16)

    e1, e2, e12 = jnp.exp(tot[1]), jnp.exp(tot[2]), jnp.exp(tot[1] + tot[2])
    if fwd:
        q1, q2, q3 = q_t[C:], q_t[2 * C:], q_t[3 * C:]
    else:
        q1, q2, q3 = q_t[:3 * C], q_t[:2 * C], q_t[:C]
    q2 = q2 * jnp.concatenate([jnp.broadcast_to(e1, (C, GLA_DK)), jnp.broadcast_to(e2, (C, GLA_DK))], axis=0)
    q3 = q3 * e12
    q_far = jnp.concatenate([q1, q2, q3], axis=0).astype(BF16)
    decay = jnp.exp(tot[0] + tot[1] + tot[2] + tot[3])
    return q_t.astype(BF16), k_t, q_far, k_hat, q_in, k_out, decay


def _gla_finish(ops, v, st, tri, dist, fwd):
    C = GLA_CHUNK
    q_t, k_t, q_far, k_hat, q_in, k_out, decay = ops
    y = _nt_dot(q_far, k_hat)
    y1, y2, y3 = y[:3 * C], y[3 * C:5 * C], y[5 * C:]
    z = lambda rows: jnp.zeros((rows, GLA_BLOCK), F32)
    if fwd:
        y1, y2, y3 = (jnp.concatenate([z(C), y1], 0), jnp.concatenate([z(2 * C), y2], 0),
                      jnp.concatenate([z(3 * C), y3], 0))
    else:
        y1, y2, y3 = (jnp.concatenate([y1, z(C)], 0), jnp.concatenate([y2, z(2 * C)], 0),
                      jnp.concatenate([y3, z(3 * C)], 0))
    x0 = _nt_dot(q_t, k_t)
    a = jnp.where(tri, x0, jnp.where(dist == 1, y1, jnp.where(dist == 2, y2, jnp.where(dist == 3, y3, 0.0))))
    o = _dot(a.astype(BF16), v) + _nt_dot(q_in, st.astype(BF16))
    st_new = st * decay + _tn_dot(v, k_out)
    return o, st_new


def _gla_dir_kernel(*refs, L, fwd):
    if fwd:
        q_ref, k_ref, v_ref, r_ref, w_ref, b_ref, o_ref, st_ref = refs
    else:
        q_ref, k_ref, v_ref, r_ref, w_ref, b_ref, g_ref, of_ref, gn_ref, o_ref, st_ref = refs
    nb = L // GLA_BLOCK

    @pl.when(pl.program_id(1) == 0)
    def _():
        st_ref[...] = jnp.zeros_like(st_ref)

    tri, dist = _gla_masks(fwd)

    def body(i, carry):
        blk = i if fwd else nb - 1 - i
        sl = pl.ds(pl.multiple_of(blk * GLA_BLOCK, GLA_BLOCK), GLA_BLOCK)
        r = r_ref[0, sl, :].astype(BF16)

        def prep(h):
            ks = slice(h * GLA_DK, (h + 1) * GLA_DK)
            return _gla_prep(q_ref[0, sl, ks], k_ref[0, sl, ks], r, w_ref[:, ks], b_ref[:, ks], tri, fwd)

        nxt = prep(0)
        for h in range(GLA_HEADS):
            vs = slice(h * GLA_DV, (h + 1) * GLA_DV)
            ops = nxt
            if h + 1 < GLA_HEADS:
                nxt = prep(h + 1)
            o, st = _gla_finish(ops, v_ref[0, sl, vs].astype(BF16), st_ref[h], tri, dist, fwd)
            st_ref[h] = st
            if fwd:
                o_ref[0, sl, vs] = o
            else:
                o = o + of_ref[0, sl, vs]
                ms = jnp.mean(o * o, axis=-1, keepdims=True)
                o = o * lax.rsqrt(ms + EPS) * gn_ref[...]
                g = g_ref[0, sl, vs]
                o_ref[0, sl, vs] = (o * (g * jax.nn.sigmoid(g))).astype(o_ref.dtype)
        return carry

    lax.fori_loop(0, nb, body, 0)


def _gla_dir(proj3, r3, w, bias, fwd, o_f=None, gn=None, *, L=512):
    B, S, _ = proj3.shape
    nl = S // L
    pos = (lambda n: n) if fwd else (lambda n: nl - 1 - n)
    full = lambda b, n: (0, 0)
    in_specs = [
        pl.BlockSpec((1, L, GLA_QK), lambda b, n: (b, pos(n), COL_GQ * LANE // GLA_QK)),
        pl.BlockSpec((1, L, GLA_QK), lambda b, n: (b, pos(n), COL_GK * LANE // GLA_QK)),
        pl.BlockSpec((1, L, GLA_V), lambda b, n: (b, pos(n), COL_GV * LANE // GLA_V)),
        pl.BlockSpec((1, L, LANE), lambda b, n: (b, pos(n), 0)),
        pl.BlockSpec((LANE, GLA_QK), full),
        pl.BlockSpec((1, GLA_QK), full),
    ]
    args = [proj3, proj3, proj3, r3, w, bias]
    if not fwd:
        in_specs += [
            pl.BlockSpec((1, L, GLA_V), lambda b, n: (b, pos(n), COL_GG * LANE // GLA_V)),
            pl.BlockSpec((1, L, GLA_V), lambda b, n: (b, pos(n), 0)),
            pl.BlockSpec((1, GLA_DV), full),
        ]
        args += [proj3, o_f, gn]
    return pl.pallas_call(
        functools.partial(_gla_dir_kernel, L=L, fwd=fwd),
        grid=(B, nl),
        in_specs=in_specs,
        out_specs=pl.BlockSpec((1, L, GLA_V), lambda b, n: (b, pos(n), 0)),
        out_shape=jax.ShapeDtypeStruct((B, S, GLA_V), F32 if fwd else BF16),
        scratch_shapes=[pltpu.VMEM((GLA_HEADS, GLA_DV, GLA_DK), F32)],
        compiler_params=_params("parallel", "arbitrary"),
        name="gla_fwd" if fwd else "gla_bwd",
    )(*args)


def _rope_tables(S):
    rows = S // GRID_W
    pos_r = np.repeat(np.arange(rows), GRID_W).astype(np.float32)
    pos_c = np.tile(np.arange(GRID_W), rows).astype(np.float32)
    half = ATT_HD // 2
    inv = jnp.asarray(ROPE_THETA, F32) ** (-jnp.arange(0, half, 2, dtype=F32) / half)
    ang = jnp.concatenate([pos_r[:, None] * inv, pos_c[:, None] * inv], axis=-1)
    cos, sin = jnp.cos(ang), jnp.sin(ang)
    return jnp.concatenate([cos, cos], axis=-1), jnp.concatenate([-sin, sin], axis=-1)


def _qkprep_kernel(aq_ref, akv_ref, cos_ref, sin_ref, qn_ref, kn_ref, q_out, k_out, v_out):
    cos = cos_ref[...]
    sin = sin_ref[...]

    def norm_rope(x, w, scale):
        ms = jnp.mean(x * x, axis=-1, keepdims=True)
        y = x * lax.rsqrt(ms + EPS) * w
        return (y * cos + pltpu.roll(y, ATT_HD // 2, 1) * sin) * scale

    for h in range(ATT_HEADS):
        sl = slice(h * ATT_HD, (h + 1) * ATT_HD)
        q_out[:, sl] = norm_rope(aq_ref[:, sl], qn_ref[...], ATT_HD ** -0.5 * LOG2_E).astype(BF16)
    for h in range(ATT_KV_HEADS):
        sl = slice(h * ATT_HD, (h + 1) * ATT_HD)
        k_out[:, sl] = norm_rope(akv_ref[:, sl], kn_ref[...], 1.0).astype(BF16)
    for h in range(ATT_KV_HEADS):
        v_out[:, 2 * h * ATT_HD:(2 * h + 1) * ATT_HD] = akv_ref[:, ATT_KV + h * ATT_HD:ATT_KV + (h + 1) * ATT_HD].astype(BF16)
        v_out[:, (2 * h + 1) * ATT_HD:(2 * h + 2) * ATT_HD] = jnp.ones((v_out.shape[0], ATT_HD), BF16)


def _qkprep(proj, cos, sin, qn, kn, S, *, tm=512):
    T = proj.shape[0]
    ns = S // tm
    return pl.pallas_call(
        _qkprep_kernel,
        grid=(T // tm,),
        in_specs=[
            pl.BlockSpec((tm, ATT_Q), lambda i: (i, COL_AQ * LANE // ATT_Q)),
            pl.BlockSpec((tm, 2 * ATT_KV), lambda i: (i, COL_AK * LANE // (2 * ATT_KV))),
            pl.BlockSpec((tm, ATT_HD), lambda i: (i % ns, 0)),
            pl.BlockSpec((tm, ATT_HD), lambda i: (i % ns, 0)),
            pl.BlockSpec((1, ATT_HD), lambda i: (0, 0)),
            pl.BlockSpec((1, ATT_HD), lambda i: (0, 0)),
        ],
        out_specs=[
            pl.BlockSpec((tm, ATT_Q), lambda i: (i, 0)),
            pl.BlockSpec((tm, ATT_KV), lambda i: (i, 0)),
            pl.BlockSpec((tm, 2 * ATT_KV), lambda i: (i, 0)),
        ],
        out_shape=[
            jax.ShapeDtypeStruct((T, ATT_Q), BF16),
            jax.ShapeDtypeStruct((T, ATT_KV), BF16),
            jax.ShapeDtypeStruct((T, 2 * ATT_KV), BF16),
        ],
        compiler_params=_params("parallel"),
        name="qkprep",
    )(proj, proj, cos, sin, qn, kn)


def _flash_kernel(q_ref, k_ref, v_ref, o_ref, m_ref, acc_ref, *, tq):
    ki = pl.program_id(3)
    tk = k_ref.shape[1]

    @pl.when(ki == 0)
    def _():
        m_ref[...] = jnp.full_like(m_ref, -jnp.inf)
        acc_ref[...] = jnp.zeros_like(acc_ref)

    k = k_ref[0]
    v = v_ref[0]

    def scores(g):
        return _nt_dot(q_ref[0, :, g * ATT_HD:(g + 1) * ATT_HD], k)

    def update(g, s):
        part = s[:, :LANE]
        for j in range(1, tk // LANE):
            part = jnp.maximum(part, s[:, j * LANE:(j + 1) * LANE])
        m = m_ref[g]
        m_new = jnp.maximum(m, jnp.max(part, axis=-1, keepdims=True))
        p = jnp.exp2(s - m_new).astype(BF16)
        acc_ref[g] = jnp.exp2(m - m_new) * acc_ref[g] + _dot(p, v)
        m_ref[g] = m_new

    s_next = scores(0)
    for g in range(ATT_GROUP):
        s = s_next
        if g + 1 < ATT_GROUP:
            s_next = scores(g + 1)
        update(g, s)

    @pl.when(ki == pl.num_programs(3) - 1)
    def _():
        for g in range(ATT_GROUP):
            acc = acc_ref[g]
            out = acc[:, :ATT_HD] / acc[:, ATT_HD:ATT_HD + 1]
            for u in range(tq // Q_BLOCK):
                rows = slice(u * Q_BLOCK, (u + 1) * Q_BLOCK)
                o_ref[0, g, :, u * ATT_HD:(u + 1) * ATT_HD] = out[rows, :].astype(o_ref.dtype)


def _flash(q3, k3, v3):
    B, S, _ = q3.shape
    gw = ATT_GROUP * ATT_HD
    tk = min(FLASH_TK, S)
    tq = min(FLASH_SCORES // tk, S)
    n_qt = S // tq
    return pl.pallas_call(
        functools.partial(_flash_kernel, tq=tq),
        grid=(B, ATT_KV_HEADS, S // tq, S // tk),
        in_specs=[
            pl.BlockSpec((1, tq, gw), lambda b, h, qi, ki: (b, qi, h)),
            pl.BlockSpec((1, tk, ATT_HD), lambda b, h, qi, ki: (b, ki, h)),
            pl.BlockSpec((1, tk, 2 * ATT_HD), lambda b, h, qi, ki: (b, ki, h)),
        ],
        out_specs=pl.BlockSpec((1, ATT_GROUP, Q_BLOCK, tq), lambda b, h, qi, ki: (b, 0, 0, h * n_qt + qi)),
        out_shape=jax.ShapeDtypeStruct((B, ATT_GROUP, Q_BLOCK, ATT_KV_HEADS * S), BF16),
        scratch_shapes=[
            pltpu.VMEM((ATT_GROUP, tq, 1), F32),
            pltpu.VMEM((ATT_GROUP, tq, 2 * ATT_HD), F32),
        ],
        compiler_params=_params("parallel", "parallel", "parallel", "arbitrary"),
        name="flash",
    )(q3, k3, v3)


def _merge_kernel(x_ref, oa_ref, ob_ref, za_ref, zb_ref, woa_ref, wob_ref, wout_ref, nf_ref, wr_ref,
                  x1_ref, h2_ref, aff_ref):
    ya = _dot(oa_ref[...], woa_ref[...])
    yb = _dot(ob_ref[...], wob_ref[...])
    merged = jax.nn.sigmoid(za_ref[...]) * ya + jax.nn.sigmoid(zb_ref[...]) * yb
    x1 = x_ref[...] + _dot(merged.astype(BF16), wout_ref[...])
    x1_ref[...] = x1
    ms = jnp.mean(x1 * x1, axis=-1, keepdims=True)
    h2 = x1 * lax.rsqrt(ms + EPS) * nf_ref[...]
    h2_ref[...] = h2.astype(BF16)
    logits = _nt_dot(wr_ref[...], h2, precision=lax.Precision.HIGHEST)
    e = jnp.exp(logits - jnp.max(logits, axis=0, keepdims=True))
    aff_ref[...] = e / jnp.sum(e, axis=0, keepdims=True)


def _merge(x, o_a, o_b, proj, w_oa, w_ob, w_out, nf, w_rt, *, tm=512):
    T = x.shape[0]
    full = lambda i: (0, 0)
    return pl.pallas_call(
        _merge_kernel,
        grid=(T // tm,),
        in_specs=[
            pl.BlockSpec((tm, D_MODEL), lambda i: (i, 0)),
            pl.BlockSpec((tm, GLA_V), lambda i: (i, 0)),
            pl.BlockSpec((tm, ATT_Q), lambda i: (i, 0)),
            pl.BlockSpec((tm, D_MODEL), lambda i: (i, COL_ZA * LANE // D_MODEL)),
            pl.BlockSpec((tm, D_MODEL), lambda i: (i, COL_ZB * LANE // D_MODEL)),
            pl.BlockSpec((GLA_V, D_MODEL), full),
            pl.BlockSpec((ATT_Q, D_MODEL), full),
            pl.BlockSpec((D_MODEL, D_MODEL), full),
            pl.BlockSpec((1, D_MODEL), full),
            pl.BlockSpec((N_EXPERTS, D_MODEL), full),
        ],
        out_specs=[
            pl.BlockSpec((tm, D_MODEL), lambda i: (i, 0)),
            pl.BlockSpec((tm, D_MODEL), lambda i: (i, 0)),
            pl.BlockSpec((N_EXPERTS, tm), lambda i: (0, i)),
        ],
        out_shape=[
            jax.ShapeDtypeStruct((T, D_MODEL), F32),
            jax.ShapeDtypeStruct((T, D_MODEL), BF16),
            jax.ShapeDtypeStruct((N_EXPERTS, T), F32),
        ],
        compiler_params=_params("parallel"),
        name="merge",
    )(x, o_a, o_b, proj, proj, w_oa, w_ob, w_out, nf, w_rt)


def _lane_cumsum(x):
    n = x.shape[1]
    lane = lax.broadcasted_iota(I32, x.shape, 1)
    shift = 1
    while shift < n:
        x = x + jnp.where(lane >= shift, pltpu.roll(x, shift, 1), 0)
        shift *= 2
    return x


def _select_kernel(aff_ref, slot_ref, incl_ref, *, cap):
    aff = aff_ref[...]

    def at_least_cap(value):
        return jnp.sum((aff >= value).astype(I32), axis=1, keepdims=True) >= cap

    def body(i, thr):
        cand = thr | jnp.left_shift(jnp.int32(1), 30 - i)
        return jnp.where(at_least_cap(pltpu.bitcast(cand, F32)), cand, thr)

    thr = lax.fori_loop(0, 31, body, jnp.zeros((aff.shape[0], 1), I32))

    lo = pltpu.bitcast(thr, F32).astype(aff.dtype)
    hi = pltpu.bitcast(jnp.maximum(thr + 1, MIN_NORMAL_F32_BITS), F32).astype(aff.dtype)

    def refine(_, bracket):
        lo, hi = bracket
        mid = 0.5 * (lo + hi)
        enough = at_least_cap(mid)
        return jnp.where(enough, mid, lo), jnp.where(enough, hi, mid)

    lo, hi = lax.fori_loop(0, SELECT_REFINE_STEPS, refine, (lo, hi))
    above = aff >= hi
    need = cap - jnp.sum(above.astype(I32), axis=1, keepdims=True)
    ties = (aff >= lo) & (aff < hi)
    sel = above | (ties & (_lane_cumsum(ties.astype(I32)) <= need))
    incl = _lane_cumsum(sel.astype(I32))
    slot_ref[...] = jnp.where(sel, incl - 1, -1)
    incl_ref[...] = incl


def _select(aff_t, cap):
    E, T = aff_t.shape
    full = lambda: (0, 0)
    slot, incl = pl.pallas_call(
        functools.partial(_select_kernel, cap=cap),
        in_specs=[pl.BlockSpec((E, T), full)],
        out_specs=[pl.BlockSpec((E, T), full), pl.BlockSpec((E, T), full)],
        out_shape=[jax.ShapeDtypeStruct((E, T), I32), jax.ShapeDtypeStruct((E, T), I32)],
        compiler_params=pltpu.CompilerParams(vmem_limit_bytes=VMEM_LIMIT),
        name="select",
    )(aff_t)
    def before(step):
        return jnp.concatenate([jnp.zeros((E, 1), I32), incl[:, step - 1::step]], axis=1)

    return slot, before(GATHER_TC), before(ROUTE_TB)


def _gather_kernel(offs_ref, slot_ref, gate_ref, h_ref, xe_ref, gc_ref, acc_ref, g_ref, ptr_ref):
    e = pl.program_id(0)
    base = pl.program_id(1) * GATHER_WIN

    first = lax.while_loop(lambda c: offs_ref[e, c + 1] <= base, lambda c: c + 1,
                           jnp.where(base == 0, 0, ptr_ref[0]))
    ptr_ref[0] = first
    stop = lax.while_loop(lambda c: offs_ref[e, c] < base + GATHER_WIN, lambda c: c + 1, first)

    def picked(slots, gates, rows):
        want = lax.broadcasted_iota(I32, (GATHER_WIN, slots.shape[1]), 0) + base
        hit = slots == want
        return (_dot(jnp.where(hit, 1.0, 0.0).astype(BF16), h_ref[rows, :]),
                jnp.sum(jnp.where(hit, gates, 0.0), axis=1, keepdims=True))

    c0 = jnp.minimum(first, slot_ref.shape[1] - GATHER_SPAN)
    span = lambda ref: jnp.concatenate([ref[0, pl.ds(c0 + j, 1), :] for j in range(GATHER_SPAN)], axis=1)
    acc_ref[...], g_ref[...] = picked(span(slot_ref), span(gate_ref),
                                      pl.ds(pl.multiple_of(c0 * GATHER_TC, GATHER_TC), GATHER_SPAN * GATHER_TC))

    def body(c, carry):
        acc, g = picked(slot_ref[0, pl.ds(c, 1), :], gate_ref[0, pl.ds(c, 1), :],
                        pl.ds(pl.multiple_of(c * GATHER_TC, GATHER_TC), GATHER_TC))
        acc_ref[...] += acc
        g_ref[...] += g
        return carry

    lax.fori_loop(c0 + GATHER_SPAN, stop, body, 0)
    xe_ref[0] = acc_ref[...].astype(xe_ref.dtype)
    gc_ref[0] = g_ref[...]


def _gather(offs, slot, aff_t, h2, cap):
    E, T = slot.shape
    n_chunks = T // GATHER_TC
    assert cap % GATHER_WIN == 0 and T % GATHER_TC == 0 and n_chunks >= GATHER_SPAN
    row_spec = pl.BlockSpec((1, n_chunks, GATHER_TC), lambda e, w, offs: (e, 0, 0))
    return pl.pallas_call(
        _gather_kernel,
        grid_spec=pltpu.PrefetchScalarGridSpec(
            num_scalar_prefetch=1,
            grid=(E, cap // GATHER_WIN),
            in_specs=[row_spec, row_spec,
                      pl.BlockSpec((T, D_MODEL), lambda e, w, offs: (0, 0), pipeline_mode=pl.Buffered(1))],
            out_specs=[
                pl.BlockSpec((1, GATHER_WIN, D_MODEL), lambda e, w, offs: (e, w, 0)),
                pl.BlockSpec((1, GATHER_WIN, 1), lambda e, w, offs: (e, w, 0)),
            ],
            scratch_shapes=[pltpu.VMEM((GATHER_WIN, D_MODEL), F32), pltpu.VMEM((GATHER_WIN, 1), F32),
                            pltpu.SMEM((1,), I32)],
        ),
        out_shape=[
            jax.ShapeDtypeStruct((E, cap, D_MODEL), BF16),
            jax.ShapeDtypeStruct((E, cap, 1), F32),
        ],
        compiler_params=_params("arbitrary", "arbitrary"),
        name="gather",
    )(offs, slot.reshape(E, n_chunks, GATHER_TC), aff_t.reshape(E, n_chunks, GATHER_TC), h2)


def _scatter_window(offs_ref, e, t, cap, level):
    rows, align = level
    return pl.multiple_of(jnp.minimum((offs_ref[e, t] // align) * align, cap - rows), align)


def _scatter_copy(offs_ref, y_hbm, ybuf, sem, e, t, buf, cap, level):
    start = _scatter_window(offs_ref, e, t, cap, level)
    rows = level[0]
    return pltpu.make_async_copy(y_hbm.at[e, pl.ds(start, rows), :],
                                 ybuf.at[buf, pl.ds(e * rows, rows), :], sem.at[buf])


def _scatter_fits(offs_ref, t, level):
    rows, align = level
    fits = True
    for e in range(N_EXPERTS):
        off = offs_ref[e, t]
        fits = fits & ((off % align) + (offs_ref[e, t + 1] - off) <= rows)
    return fits


def _scatter_kernel(offs_ref, slot_ref, x1_ref, y_hbm, o_ref, ybuf, sem, *, cap, levels):
    t = pl.program_id(0)
    buf = t % 2

    def by_level(tt, fn):
        taken = False
        for level in levels[:-1]:
            use = jnp.logical_not(taken) & _scatter_fits(offs_ref, tt, level)
            pl.when(use)(functools.partial(fn, level))
            taken = taken | use
        pl.when(jnp.logical_not(taken))(functools.partial(fn, levels[-1]))

    def start(tt, b, level):
        for e in range(N_EXPERTS):
            _scatter_copy(offs_ref, y_hbm, ybuf, sem, e, tt, b, cap, level).start()

    @pl.when(t == 0)
    def _():
        by_level(t, functools.partial(start, t, buf))

    @pl.when(t + 1 < pl.num_programs(0))
    def _():
        by_level(t + 1, functools.partial(start, t + 1, 1 - buf))

    def finish(level):
        rows = level[0]
        lane = lax.broadcasted_iota(I32, (ROUTE_TB, rows), 1)
        hits = []
        for e in range(N_EXPERTS):
            rel = slot_ref[:, e:e + 1] - _scatter_window(offs_ref, e, t, cap, level)
            hits.append(jnp.where(rel == lane, 1.0, 0.0).astype(BF16))
        onehot = jnp.concatenate(hits, axis=1)
        for e in range(N_EXPERTS):
            _scatter_copy(offs_ref, y_hbm, ybuf, sem, e, t, buf, cap, level).wait()
        o_ref[...] = x1_ref[...] + _dot(onehot, ybuf[buf, pl.ds(0, N_EXPERTS * rows), :])

    by_level(t, finish)


def _scatter(offs, slot, x1, ye):
    E, T = slot.shape
    cap = ye.shape[1]
    levels = tuple((min(r, cap), a) for r, a in SCATTER_LEVELS)
    rows = levels[-1][0]
    return pl.pallas_call(
        functools.partial(_scatter_kernel, cap=cap, levels=levels),
        grid_spec=pltpu.PrefetchScalarGridSpec(
            num_scalar_prefetch=1,
            grid=(T // ROUTE_TB,),
            in_specs=[
                pl.BlockSpec((ROUTE_TB, E), lambda t, offs: (t, 0)),
                pl.BlockSpec((ROUTE_TB, D_MODEL), lambda t, offs: (t, 0)),
                pl.BlockSpec(memory_space=pl.ANY),
            ],
            out_specs=pl.BlockSpec((ROUTE_TB, D_MODEL), lambda t, offs: (t, 0)),
            scratch_shapes=[pltpu.VMEM((2, E * rows, D_MODEL), BF16), pltpu.SemaphoreType.DMA((2,))],
        ),
        out_shape=jax.ShapeDtypeStruct((T, D_MODEL), F32),
        compiler_params=_params("arbitrary"),
        name="scatter",
    )(offs, slot.T, x1, ye)


def _ffn_kernel(xe_ref, gate_ref, wg_ref, wu_ref, wd_ref, o_ref, acc_ref, *, rows):
    f = pl.program_id(1)
    cap = xe_ref.shape[1]

    @pl.when(f == 0)
    def _():
        acc_ref[...] = jnp.zeros_like(acc_ref)

    wg = wg_ref[0].astype(BF16)
    wu = wu_ref[0].astype(BF16)

    def up(i):
        xe = xe_ref[0, i * rows:(i + 1) * rows, :]
        return _dot(xe, wg), _dot(xe, wu)

    nxt = up(0)
    wd = wd_ref[0].astype(BF16)
    for i in range(cap // rows):
        hg, hu = nxt
        if i + 1 < cap // rows:
            nxt = up(i + 1)
        hid = (hg * jax.nn.sigmoid(hg) * hu).astype(BF16)
        acc_ref[i * rows:(i + 1) * rows, :] += _dot(hid, wd)

    @pl.when(f == pl.num_programs(1) - 1)
    def _():
        o_ref[0] = (acc_ref[...] * gate_ref[0]).astype(o_ref.dtype)


def _ffn(xe, gate, w_g, w_u, w_d, *, tf=512, rows=1024):
    E, cap, _ = xe.shape
    rows = min(rows, cap)
    return pl.pallas_call(
        functools.partial(_ffn_kernel, rows=rows),
        grid=(E, EXPERT_FF // tf),
        in_specs=[
            pl.BlockSpec((1, cap, D_MODEL), lambda e, f: (e, 0, 0)),
            pl.BlockSpec((1, cap, 1), lambda e, f: (e, 0, 0)),
            pl.BlockSpec((1, D_MODEL, tf), lambda e, f: (e, 0, f)),
            pl.BlockSpec((1, D_MODEL, tf), lambda e, f: (e, 0, f)),
            pl.BlockSpec((1, tf, D_MODEL), lambda e, f: (e, f, 0)),
        ],
        out_specs=pl.BlockSpec((1, cap, D_MODEL), lambda e, f: (e, 0, 0)),
        out_shape=jax.ShapeDtypeStruct((E, cap, D_MODEL), BF16),
        scratch_shapes=[pltpu.VMEM((cap, D_MODEL), F32)],
        compiler_params=_params("parallel", "arbitrary"),
        name="ffn",
    )(xe, gate, w_g, w_u, w_d)


def _prep_weights(w_in, w_gk_f, w_gk_b, q_norm, k_norm):
    pts = np.cumsum([GLA_QK, GLA_QK, GLA_V, GLA_V, GLA_RANK, GLA_RANK, ATT_Q, ATT_KV, ATT_KV, D_MODEL])
    gq, gk, gv, gg, r_f, r_b, aq, ak, av, z_a, z_b = jnp.split(w_in, pts, axis=-1)
    perm = np.concatenate([np.arange(0, ATT_HD, 2), np.arange(1, ATT_HD, 2)])
    perm_q = (np.arange(ATT_HEADS)[:, None] * ATT_HD + perm[None, :]).reshape(-1)
    perm_k = (np.arange(ATT_KV_HEADS)[:, None] * ATT_HD + perm[None, :]).reshape(-1)
    w_main = jnp.concatenate([gq, gk, gv, gg, z_a, z_b, aq[:, perm_q], ak[:, perm_k], av], axis=-1)
    w_r = jnp.pad(jnp.concatenate([r_f, r_b], axis=-1), ((0, 0), (0, LANE - 2 * GLA_RANK)))
    wf = jnp.pad(w_gk_f, ((0, LANE - GLA_RANK), (0, 0)))
    wb = jnp.pad(w_gk_b, ((GLA_RANK, LANE - 2 * GLA_RANK), (0, 0)))
    return (w_main.astype(BF16), w_r.astype(BF16), wf.astype(BF16), wb.astype(BF16),
            q_norm[perm][None, :], k_norm[perm][None, :])


def _layer(x3, norm_mix, w_main, w_r, wf, bf, wb, bb, gla_norm, qn, kn,
           w_oa, w_ob, w_out, norm_ffn, w_rt, w_e_gate, w_e_up, w_e_down):
    B, S, D = x3.shape
    T = B * S
    x = x3.reshape(T, D)
    proj, r = _inproj(x, norm_mix, w_r, w_main)
    proj3, r3 = proj.reshape(B, S, -1), r.reshape(B, S, LANE)
    o_f = _gla_dir(proj3, r3, wf, bf, True)
    o_a = _gla_dir(proj3, r3, wb, bb, False, o_f, gla_norm)
    cos, sin = _rope_tables(S)
    q, k, v = _qkprep(proj, cos, sin, qn, kn, S)
    o_b = _flash(q.reshape(B, S, ATT_Q), k.reshape(B, S, ATT_KV), v.reshape(B, S, 2 * ATT_KV))
    x1, h2, aff_t = _merge(x, o_a.reshape(T, GLA_V), o_b.reshape(T, ATT_Q), proj,
                           w_oa, w_ob, w_out, norm_ffn, w_rt)
    cap = CAPACITY_FACTOR * T // N_EXPERTS
    slot, offs_gather, offs_scatter = _select(aff_t, cap)
    xe, gate = _gather(offs_gather, slot, aff_t, h2, cap)
    ye = _ffn(xe, gate, w_e_gate, w_e_up, w_e_down)
    out = _scatter(offs_scatter, slot, x1, ye)
    return out.reshape(B, S, D)


def kernel(x_prompt, x_sample, norm_mix, w_in, w_gk_f, b_gk_f, w_gk_b, b_gk_b, gla_norm, q_norm, k_norm,
           w_o_gla, w_o_att, w_out, norm_ffn, w_router, w_e_gate, w_e_up, w_e_down):
    w_main, w_r, wf, wb, qn, kn = _prep_weights(w_in[0], w_gk_f[0], w_gk_b[0], q_norm[0], k_norm[0])
    args = (norm_mix[0][None, :], w_main, w_r, wf, b_gk_f[0][None, :], wb, b_gk_b[0][None, :],
            gla_norm[0][None, :], qn, kn,
            w_o_gla[0].astype(BF16), w_o_att[0].astype(BF16), w_out[0].astype(BF16),
            norm_ffn[0][None, :], w_router[0].T, w_e_gate[0], w_e_up[0], w_e_down[0])
    return (_layer(x_prompt, *args), _layer(x_sample, *args))
```

```python
import functools

import jax
import jax.numpy as jnp
import numpy as np
from jax import lax
from jax.experimental import pallas as pl
from jax.experimental.pallas import tpu as pltpu

D_MODEL = 1024
GRID_W = 64
GLA_HEADS = 4
GLA_DK = 128
GLA_DV = 256
GLA_RANK = 16
GLA_GATE_NORM = 16.0
GLA_CHUNK = 64
GLA_QK = GLA_HEADS * GLA_DK
GLA_V = GLA_HEADS * GLA_DV
ATT_HEADS = 8
ATT_KV_HEADS = 2
ATT_GROUP = ATT_HEADS // ATT_KV_HEADS
ATT_HD = 128
ATT_Q = ATT_HEADS * ATT_HD
ATT_KV = ATT_KV_HEADS * ATT_HD
ROPE_THETA = 10000.0
Q_BLOCK = 128
N_EXPERTS = 16
CAPACITY_FACTOR = 2
EXPERT_FF = 2048
EPS = 1e-6
LOG2_E = 1.4426950408889634

LANE = 128
VMEM_LIMIT = 56 * 1024 * 1024

BF16 = jnp.bfloat16
F32 = jnp.float32
I32 = jnp.int32

COL_GQ, COL_GK, COL_GV, COL_GG = 0, 4, 8, 16
COL_ZA, COL_ZB, COL_AQ, COL_AK, COL_AV = 24, 32, 40, 48, 50
N_MAIN_COLS = 52 * LANE

GLA_BLOCK = 256
GLA_SUB = GLA_BLOCK // GLA_CHUNK
ROUTE_TB = 256
ROUTE_WIN = 128
ROUTE_NWIN = ROUTE_TB // ROUTE_WIN + 1
SCATTER_LEVELS = ((ROUTE_WIN, ROUTE_WIN // 2), (ROUTE_TB, ROUTE_WIN), (ROUTE_NWIN * ROUTE_WIN, ROUTE_WIN))
FLASH_TK = 4096
FLASH_SCORES = 512 * 4096
MIN_NORMAL_F32_BITS = 0x00800000
SELECT_REFINE_STEPS = 32
GATHER_TC = 256
GATHER_WIN = 128
GATHER_SPAN = 5


def _nt_dot(a, b, **kw):
    return lax.dot_general(a, b, (((1,), (1,)), ((), ())), preferred_element_type=F32, **kw)


def _tn_dot(a, b):
    return lax.dot_general(a, b, (((0,), (0,)), ((), ())), preferred_element_type=F32)


def _dot(a, b):
    return jnp.dot(a, b, preferred_element_type=F32)


def _params(*sem):
    return pltpu.CompilerParams(dimension_semantics=sem, vmem_limit_bytes=VMEM_LIMIT)


def _inproj_kernel(x_ref, g_ref, wr_ref, w_ref, o_ref, r_ref, hn_ref):
    @pl.when(pl.program_id(1) == 0)
    def _():
        x = x_ref[...]
        ms = jnp.mean(x * x, axis=-1, keepdims=True)
        hn = (x * lax.rsqrt(ms + EPS) * g_ref[...]).astype(BF16)
        hn_ref[...] = hn
        r_ref[...] = _dot(hn, wr_ref[...])

    o_ref[...] = _dot(hn_ref[...], w_ref[...])


def _inproj(x, g, w_r, w_main, *, tm=1024, n_col_tiles=4):
    T = x.shape[0]
    tn = N_MAIN_COLS // n_col_tiles
    return pl.pallas_call(
        _inproj_kernel,
        grid=(T // tm, n_col_tiles),
        in_specs=[
            pl.BlockSpec((tm, D_MODEL), lambda i, j: (i, 0)),
            pl.BlockSpec((1, D_MODEL), lambda i, j: (0, 0)),
            pl.BlockSpec((D_MODEL, LANE), lambda i, j: (0, 0)),
            pl.BlockSpec((D_MODEL, tn), lambda i, j: (0, j)),
        ],
        out_specs=[
            pl.BlockSpec((tm, tn), lambda i, j: (i, j)),
            pl.BlockSpec((tm, LANE), lambda i, j: (i, 0)),
        ],
        out_shape=[
            jax.ShapeDtypeStruct((T, N_MAIN_COLS), F32),
            jax.ShapeDtypeStruct((T, LANE), F32),
        ],
        scratch_shapes=[pltpu.VMEM((tm, D_MODEL), BF16)],
        compiler_params=_params("parallel", "arbitrary"),
        name="inproj",
    )(x, g, w_r, w_main)


def _log_sigmoid(x):
    return jnp.minimum(x, 0.0) - jnp.log(1.0 + jnp.exp(-jnp.abs(x)))


def _rows(parts):
    return jnp.concatenate([jnp.broadcast_to(p, (GLA_CHUNK, p.shape[-1])) for p in parts], axis=0)


def _gla_masks(fwd):
    n = GLA_BLOCK
    row = lax.broadcasted_iota(I32, (n, n), 0)
    col = lax.broadcasted_iota(I32, (n, n), 1)
    same = (row // GLA_CHUNK) == (col // GLA_CHUNK)
    tri = same & ((col <= row) if fwd else (col >= row))
    dist = (row // GLA_CHUNK - col // GLA_CHUNK) * (1 if fwd else -1)
    return tri, dist


def _gla_prep(q, k, r, w, bias, tri, fwd):
    C = GLA_CHUNK
    lg = _log_sigmoid(_dot(r, w) + bias) * (1.0 / GLA_GATE_NORM)
    hi = lg.astype(BF16)
    rem = lg - hi.astype(F32)
    mid = rem.astype(BF16)
    lo = (rem - mid.astype(F32)).astype(BF16)
    tri_b = jnp.where(tri, 1.0, 0.0).astype(BF16)
    b = _dot(tri_b, hi) + _dot(tri_b, mid) + _dot(tri_b, lo)
    last = (lambda s: s * C + C - 1) if fwd else (lambda s: s * C)
    tot = [b[last(s):last(s) + 1, :] for s in range(GLA_SUB)]
    zero = jnp.zeros_like(tot[0])
    before = [zero, tot[0], tot[0] + tot[1], tot[0] + tot[1] + tot[2]]
    after = [tot[1] + tot[2] + tot[3], tot[2] + tot[3], tot[3], zero]
    lead, trail = (before, after) if fwd else (after, before)

    q_t = q * jnp.exp(b) * (GLA_DK ** -0.5)
    k_t = (k * jnp.exp(-b)).astype(BF16)
    k_hat = k * jnp.exp(_rows(tot) - b)
    q_in = (q_t * _rows([jnp.exp(x) for x in lead])).astype(BF16)
    k_out = (k_hat * _rows([jnp.exp(x) for x in trail])).astype(BF16)
    k_hat = k_hat.astype(BF16)

    e1, e2, e12 = jnp.exp(tot[1]), jnp.exp(tot[2]), jnp.exp(tot[1] + tot[2])
    if fwd:
        q1, q2, q3 = q_t[C:], q_t[2 * C:], q_t[3 * C:]
    else:
        q1, q2, q3 = q_t[:3 * C], q_t[:2 * C], q_t[:C]
    q2 = q2 * jnp.concatenate([jnp.broadcast_to(e1, (C, GLA_DK)), jnp.broadcast_to(e2, (C, GLA_DK))], axis=0)
    q3 = q3 * e12
    q_far = jnp.concatenate([q1, q2, q3], axis=0).astype(BF16)
    decay = jnp.exp(tot[0] + tot[1] + tot[2] + tot[3])
    return q_t.astype(BF16), k_t, q_far, k_hat, q_in, k_out, decay


def _gla_finish(ops, v, st, tri, dist, fwd):
    C = GLA_CHUNK
    q_t, k_t, q_far, k_hat, q_in, k_out, decay = ops
    y = _nt_dot(q_far, k_hat)
    y1, y2, y3 = y[:3 * C], y[3 * C:5 * C], y[5 * C:]
    z = lambda rows: jnp.zeros((rows, GLA_BLOCK), F32)
    if fwd:
        y1, y2, y3 = (jnp.concatenate([z(C), y1], 0), jnp.concatenate([z(2 * C), y2], 0),
                      jnp.concatenate([z(3 * C), y3], 0))
    else:
        y1, y2, y3 = (jnp.concatenate([y1, z(C)], 0), jnp.concatenate([y2, z(2 * C)], 0),
                      jnp.concatenate([y3, z(3 * C)], 0))
    x0 = _nt_dot(q_t, k_t)
    a = jnp.where(tri, x0, jnp.where(dist == 1, y1, jnp.where(dist == 2, y2, jnp.where(dist == 3, y3, 0.0))))
    o = _dot(a.astype(BF16), v) + _nt_dot(q_in, st.astype(BF16))
    st_new = st * decay + _tn_dot(v, k_out)
    return o, st_new


def _gla_dir_kernel(*refs, L, fwd):
    if fwd:
        q_ref, k_ref, v_ref, r_ref, w_ref, b_ref, o_ref, st_ref = refs
    else:
        q_ref, k_ref, v_ref, r_ref, w_ref, b_ref, g_ref, of_ref, gn_ref, o_ref, st_ref = refs
    nb = L // GLA_BLOCK

    @pl.when(pl.program_id(1) == 0)
    def _():
        st_ref[...] = jnp.zeros_like(st_ref)

    tri, dist = _gla_masks(fwd)

    def body(i, carry):
        blk = i if fwd else nb - 1 - i
        sl = pl.ds(pl.multiple_of(blk * GLA_BLOCK, GLA_BLOCK), GLA_BLOCK)
        r = r_ref[0, sl, :].astype(BF16)

        def prep(h):
            ks = slice(h * GLA_DK, (h + 1) * GLA_DK)
            return _gla_prep(q_ref[0, sl, ks], k_ref[0, sl, ks], r, w_ref[:, ks], b_ref[:, ks], tri, fwd)

        nxt = prep(0)
        for h in range(GLA_HEADS):
            vs = slice(h * GLA_DV, (h + 1) * GLA_DV)
            ops = nxt
            if h + 1 < GLA_HEADS:
                nxt = prep(h + 1)
            o, st = _gla_finish(ops, v_ref[0, sl, vs].astype(BF16), st_ref[h], tri, dist, fwd)
            st_ref[h] = st
            if fwd:
                o_ref[0, sl, vs] = o
            else:
                o = o + of_ref[0, sl, vs]
                ms = jnp.mean(o * o, axis=-1, keepdims=True)
                o = o * lax.rsqrt(ms + EPS) * gn_ref[...]
                g = g_ref[0, sl, vs]
                o_ref[0, sl, vs] = (o * (g * jax.nn.sigmoid(g))).astype(o_ref.dtype)
        return carry

    lax.fori_loop(0, nb, body, 0)


def _gla_dir(proj3, r3, w, bias, fwd, o_f=None, gn=None, *, L=1024):
    B, S, _ = proj3.shape
    nl = S // L
    pos = (lambda n: n) if fwd else (lambda n: nl - 1 - n)
    full = lambda b, n: (0, 0)
    in_specs = [
        pl.BlockSpec((1, L, GLA_QK), lambda b, n: (b, pos(n), COL_GQ * LANE // GLA_QK)),
        pl.BlockSpec((1, L, GLA_QK), lambda b, n: (b, pos(n), COL_GK * LANE // GLA_QK)),
        pl.BlockSpec((1, L, GLA_V), lambda b, n: (b, pos(n), COL_GV * LANE // GLA_V)),
        pl.BlockSpec((1, L, LANE), lambda b, n: (b, pos(n), 0)),
        pl.BlockSpec((LANE, GLA_QK), full),
        pl.BlockSpec((1, GLA_QK), full),
    ]
    args = [proj3, proj3, proj3, r3, w, bias]
    if not fwd:
        in_specs += [
            pl.BlockSpec((1, L, GLA_V), lambda b, n: (b, pos(n), COL_GG * LANE // GLA_V)),
            pl.BlockSpec((1, L, GLA_V), lambda b, n: (b, pos(n), 0)),
            pl.BlockSpec((1, GLA_DV), full),
        ]
        args += [proj3, o_f, gn]
    return pl.pallas_call(
        functools.partial(_gla_dir_kernel, L=L, fwd=fwd),
        grid=(B, nl),
        in_specs=in_specs,
        out_specs=pl.BlockSpec((1, L, GLA_V), lambda b, n: (b, pos(n), 0)),
        out_shape=jax.ShapeDtypeStruct((B, S, GLA_V), F32 if fwd else BF16),
        scratch_shapes=[pltpu.VMEM((GLA_HEADS, GLA_DV, GLA_DK), F32)],
        compiler_params=_params("parallel", "arbitrary"),
        name="gla_fwd" if fwd else "gla_bwd",
    )(*args)


def _rope_tables(S):
    rows = S // GRID_W
    pos_r = np.repeat(np.arange(rows), GRID_W).astype(np.float32)
    pos_c = np.tile(np.arange(GRID_W), rows).astype(np.float32)
    half = ATT_HD // 2
    inv = jnp.asarray(ROPE_THETA, F32) ** (-jnp.arange(0, half, 2, dtype=F32) / half)
    ang = jnp.concatenate([pos_r[:, None] * inv, pos_c[:, None] * inv], axis=-1)
    cos, sin = jnp.cos(ang), jnp.sin(ang)
    return jnp.concatenate([cos, cos], axis=-1), jnp.concatenate([-sin, sin], axis=-1)


def _qkprep_kernel(aq_ref, akv_ref, cos_ref, sin_ref, qn_ref, kn_ref, q_out, k_out, v_out):
    cos = cos_ref[...]
    sin = sin_ref[...]

    def norm_rope(x, w, scale):
        ms = jnp.mean(x * x, axis=-1, keepdims=True)
        y = x * lax.rsqrt(ms + EPS) * w
        return (y * cos + pltpu.roll(y, ATT_HD // 2, 1) * sin) * scale

    for h in range(ATT_HEADS):
        sl = slice(h * ATT_HD, (h + 1) * ATT_HD)
        q_out[:, sl] = norm_rope(aq_ref[:, sl], qn_ref[...], ATT_HD ** -0.5 * LOG2_E).astype(BF16)
    for h in range(ATT_KV_HEADS):
        sl = slice(h * ATT_HD, (h + 1) * ATT_HD)
        k_out[:, sl] = norm_rope(akv_ref[:, sl], kn_ref[...], 1.0).astype(BF16)
    for h in range(ATT_KV_HEADS):
        v_out[:, 2 * h * ATT_HD:(2 * h + 1) * ATT_HD] = akv_ref[:, ATT_KV + h * ATT_HD:ATT_KV + (h + 1) * ATT_HD].astype(BF16)
        v_out[:, (2 * h + 1) * ATT_HD:(2 * h + 2) * ATT_HD] = jnp.ones((v_out.shape[0], ATT_HD), BF16)


def _qkprep(proj, cos, sin, qn, kn, S, *, tm=1024):
    T = proj.shape[0]
    ns = S // tm
    return pl.pallas_call(
        _qkprep_kernel,
        grid=(T // tm,),
        in_specs=[
            pl.BlockSpec((tm, ATT_Q), lambda i: (i, COL_AQ * LANE // ATT_Q)),
            pl.BlockSpec((tm, 2 * ATT_KV), lambda i: (i, COL_AK * LANE // (2 * ATT_KV))),
            pl.BlockSpec((tm, ATT_HD), lambda i: (i % ns, 0)),
            pl.BlockSpec((tm, ATT_HD), lambda i: (i % ns, 0)),
            pl.BlockSpec((1, ATT_HD), lambda i: (0, 0)),
            pl.BlockSpec((1, ATT_HD), lambda i: (0, 0)),
        ],
        out_specs=[
            pl.BlockSpec((tm, ATT_Q), lambda i: (i, 0)),
            pl.BlockSpec((tm, ATT_KV), lambda i: (i, 0)),
            pl.BlockSpec((tm, 2 * ATT_KV), lambda i: (i, 0)),
        ],
        out_shape=[
            jax.ShapeDtypeStruct((T, ATT_Q), BF16),
            jax.ShapeDtypeStruct((T, ATT_KV), BF16),
            jax.ShapeDtypeStruct((T, 2 * ATT_KV), BF16),
        ],
        compiler_params=_params("parallel"),
        name="qkprep",
    )(proj, proj, cos, sin, qn, kn)


def _flash_kernel(q_ref, k_ref, v_ref, o_ref, m_ref, acc_ref, *, tq):
    ki = pl.program_id(3)
    tk = k_ref.shape[1]

    @pl.when(ki == 0)
    def _():
        m_ref[...] = jnp.full_like(m_ref, -jnp.inf)
        acc_ref[...] = jnp.zeros_like(acc_ref)

    k = k_ref[0]
    v = v_ref[0]

    def scores(g):
        return _nt_dot(q_ref[0, :, g * ATT_HD:(g + 1) * ATT_HD], k)

    def update(g, s):
        part = s[:, :LANE]
        for j in range(1, tk // LANE):
            part = jnp.maximum(part, s[:, j * LANE:(j + 1) * LANE])
        m = m_ref[g]
        m_new = jnp.maximum(m, jnp.max(part, axis=-1, keepdims=True))
        p = jnp.exp2(s - m_new).astype(BF16)
        acc_ref[g] = jnp.exp2(m - m_new) * acc_ref[g] + _dot(p, v)
        m_ref[g] = m_new

    s_next = scores(0)
    for g in range(ATT_GROUP):
        s = s_next
        if g + 1 < ATT_GROUP:
            s_next = scores(g + 1)
        update(g, s)

    @pl.when(ki == pl.num_programs(3) - 1)
    def _():
        for g in range(ATT_GROUP):
            acc = acc_ref[g]
            out = acc[:, :ATT_HD] / acc[:, ATT_HD:ATT_HD + 1]
            for u in range(tq // Q_BLOCK):
                rows = slice(u * Q_BLOCK, (u + 1) * Q_BLOCK)
                o_ref[0, g, :, u * ATT_HD:(u + 1) * ATT_HD] = out[rows, :].astype(o_ref.dtype)


def _flash(q3, k3, v3):
    B, S, _ = q3.shape
    gw = ATT_GROUP * ATT_HD
    tk = min(FLASH_TK, S)
    tq = min(FLASH_SCORES // tk, S)
    n_qt = S // tq
    return pl.pallas_call(
        functools.partial(_flash_kernel, tq=tq),
        grid=(B, ATT_KV_HEADS, S // tq, S // tk),
        in_specs=[
            pl.BlockSpec((1, tq, gw), lambda b, h, qi, ki: (b, qi, h)),
            pl.BlockSpec((1, tk, ATT_HD), lambda b, h, qi, ki: (b, ki, h)),
            pl.BlockSpec((1, tk, 2 * ATT_HD), lambda b, h, qi, ki: (b, ki, h)),
        ],
        out_specs=pl.BlockSpec((1, ATT_GROUP, Q_BLOCK, tq), lambda b, h, qi, ki: (b, 0, 0, h * n_qt + qi)),
        out_shape=jax.ShapeDtypeStruct((B, ATT_GROUP, Q_BLOCK, ATT_KV_HEADS * S), BF16),
        scratch_shapes=[
            pltpu.VMEM((ATT_GROUP, tq, 1), F32),
            pltpu.VMEM((ATT_GROUP, tq, 2 * ATT_HD), F32),
        ],
        compiler_params=_params("parallel", "parallel", "parallel", "arbitrary"),
        name="flash",
    )(q3, k3, v3)


def _merge_kernel(x_ref, oa_ref, ob_ref, za_ref, zb_ref, woa_ref, wob_ref, wout_ref, nf_ref, wr_ref,
                  x1_ref, h2_ref, aff_ref):
    ya = _dot(oa_ref[...], woa_ref[...])
    yb = _dot(ob_ref[...], wob_ref[...])
    merged = jax.nn.sigmoid(za_ref[...]) * ya + jax.nn.sigmoid(zb_ref[...]) * yb
    x1 = x_ref[...] + _dot(merged.astype(BF16), wout_ref[...])
    x1_ref[...] = x1
    ms = jnp.mean(x1 * x1, axis=-1, keepdims=True)
    h2 = x1 * lax.rsqrt(ms + EPS) * nf_ref[...]
    h2_ref[...] = h2.astype(BF16)
    logits = _nt_dot(wr_ref[...], h2, precision=lax.Precision.HIGHEST)
    e = jnp.exp(logits - jnp.max(logits, axis=0, keepdims=True))
    aff_ref[...] = e / jnp.sum(e, axis=0, keepdims=True)


def _merge(x, o_a, o_b, proj, w_oa, w_ob, w_out, nf, w_rt, *, tm=512):
    T = x.shape[0]
    full = lambda i: (0, 0)
    return pl.pallas_call(
        _merge_kernel,
        grid=(T // tm,),
        in_specs=[
            pl.BlockSpec((tm, D_MODEL), lambda i: (i, 0)),
            pl.BlockSpec((tm, GLA_V), lambda i: (i, 0)),
            pl.BlockSpec((tm, ATT_Q), lambda i: (i, 0)),
            pl.BlockSpec((tm, D_MODEL), lambda i: (i, COL_ZA * LANE // D_MODEL)),
            pl.BlockSpec((tm, D_MODEL), lambda i: (i, COL_ZB * LANE // D_MODEL)),
            pl.BlockSpec((GLA_V, D_MODEL), full),
            pl.BlockSpec((ATT_Q, D_MODEL), full),
            pl.BlockSpec((D_MODEL, D_MODEL), full),
            pl.BlockSpec((1, D_MODEL), full),
            pl.BlockSpec((N_EXPERTS, D_MODEL), full),
        ],
        out_specs=[
            pl.BlockSpec((tm, D_MODEL), lambda i: (i, 0)),
            pl.BlockSpec((tm, D_MODEL), lambda i: (i, 0)),
            pl.BlockSpec((N_EXPERTS, tm), lambda i: (0, i)),
        ],
        out_shape=[
            jax.ShapeDtypeStruct((T, D_MODEL), F32),
            jax.ShapeDtypeStruct((T, D_MODEL), BF16),
            jax.ShapeDtypeStruct((N_EXPERTS, T), F32),
        ],
        compiler_params=_params("parallel"),
        name="merge",
    )(x, o_a, o_b, proj, proj, w_oa, w_ob, w_out, nf, w_rt)


def _lane_cumsum(x):
    n = x.shape[1]
    lane = lax.broadcasted_iota(I32, x.shape, 1)
    shift = 1
    while shift < n:
        x = x + jnp.where(lane >= shift, pltpu.roll(x, shift, 1), 0)
        shift *= 2
    return x


def _select_kernel(aff_ref, slot_ref, incl_ref, *, cap):
    aff = aff_ref[...]

    def at_least_cap(value):
        return jnp.sum((aff >= value).astype(I32), axis=1, keepdims=True) >= cap

    def body(i, thr):
        cand = thr | jnp.left_shift(jnp.int32(1), 30 - i)
        return jnp.where(at_least_cap(pltpu.bitcast(cand, F32)), cand, thr)

    thr = lax.fori_loop(0, 31, body, jnp.zeros((aff.shape[0], 1), I32))

    lo = pltpu.bitcast(thr, F32).astype(aff.dtype)
    hi = pltpu.bitcast(jnp.maximum(thr + 1, MIN_NORMAL_F32_BITS), F32).astype(aff.dtype)

    def refine(_, bracket):
        lo, hi = bracket
        mid = 0.5 * (lo + hi)
        enough = at_least_cap(mid)
        return jnp.where(enough, mid, lo), jnp.where(enough, hi, mid)

    lo, hi = lax.fori_loop(0, SELECT_REFINE_STEPS, refine, (lo, hi))
    above = aff >= hi
    need = cap - jnp.sum(above.astype(I32), axis=1, keepdims=True)
    ties = (aff >= lo) & (aff < hi)
    sel = above | (ties & (_lane_cumsum(ties.astype(I32)) <= need))
    incl = _lane_cumsum(sel.astype(I32))
    slot_ref[...] = jnp.where(sel, incl - 1, -1)
    incl_ref[...] = incl


def _select(aff_t, cap):
    E, T = aff_t.shape
    full = lambda: (0, 0)
    slot, incl = pl.pallas_call(
        functools.partial(_select_kernel, cap=cap),
        in_specs=[pl.BlockSpec((E, T), full)],
        out_specs=[pl.BlockSpec((E, T), full), pl.BlockSpec((E, T), full)],
        out_shape=[jax.ShapeDtypeStruct((E, T), I32), jax.ShapeDtypeStruct((E, T), I32)],
        compiler_params=pltpu.CompilerParams(vmem_limit_bytes=VMEM_LIMIT),
        name="select",
    )(aff_t)
    def before(step):
        return jnp.concatenate([jnp.zeros((E, 1), I32), incl[:, step - 1::step]], axis=1)

    return slot, before(GATHER_TC), before(ROUTE_TB)


def _gather_kernel(offs_ref, slot_ref, gate_ref, h_ref, xe_ref, gc_ref, acc_ref, g_ref, ptr_ref):
    e = pl.program_id(0)
    base = pl.program_id(1) * GATHER_WIN

    first = lax.while_loop(lambda c: offs_ref[e, c + 1] <= base, lambda c: c + 1,
                           jnp.where(base == 0, 0, ptr_ref[0]))
    ptr_ref[0] = first
    stop = lax.while_loop(lambda c: offs_ref[e, c] < base + GATHER_WIN, lambda c: c + 1, first)

    def picked(slots, gates, rows):
        want = lax.broadcasted_iota(I32, (GATHER_WIN, slots.shape[1]), 0) + base
        hit = slots == want
        return (_dot(jnp.where(hit, 1.0, 0.0).astype(BF16), h_ref[rows, :]),
                jnp.sum(jnp.where(hit, gates, 0.0), axis=1, keepdims=True))

    c0 = jnp.minimum(first, slot_ref.shape[1] - GATHER_SPAN)
    span = lambda ref: jnp.concatenate([ref[0, pl.ds(c0 + j, 1), :] for j in range(GATHER_SPAN)], axis=1)
    acc_ref[...], g_ref[...] = picked(span(slot_ref), span(gate_ref),
                                      pl.ds(pl.multiple_of(c0 * GATHER_TC, GATHER_TC), GATHER_SPAN * GATHER_TC))

    def body(c, carry):
        acc, g = picked(slot_ref[0, pl.ds(c, 1), :], gate_ref[0, pl.ds(c, 1), :],
                        pl.ds(pl.multiple_of(c * GATHER_TC, GATHER_TC), GATHER_TC))
        acc_ref[...] += acc
        g_ref[...] += g
        return carry

    lax.fori_loop(c0 + GATHER_SPAN, stop, body, 0)
    xe_ref[0] = acc_ref[...].astype(xe_ref.dtype)
    gc_ref[0] = g_ref[...]


def _gather(offs, slot, aff_t, h2, cap):
    E, T = slot.shape
    n_chunks = T // GATHER_TC
    assert cap % GATHER_WIN == 0 and T % GATHER_TC == 0 and n_chunks >= GATHER_SPAN
    row_spec = pl.BlockSpec((1, n_chunks, GATHER_TC), lambda e, w, offs: (e, 0, 0))
    return pl.pallas_call(
        _gather_kernel,
        grid_spec=pltpu.PrefetchScalarGridSpec(
            num_scalar_prefetch=1,
            grid=(E, cap // GATHER_WIN),
            in_specs=[row_spec, row_spec,
                      pl.BlockSpec((T, D_MODEL), lambda e, w, offs: (0, 0), pipeline_mode=pl.Buffered(1))],
            out_specs=[
                pl.BlockSpec((1, GATHER_WIN, D_MODEL), lambda e, w, offs: (e, w, 0)),
                pl.BlockSpec((1, GATHER_WIN, 1), lambda e, w, offs: (e, w, 0)),
            ],
            scratch_shapes=[pltpu.VMEM((GATHER_WIN, D_MODEL), F32), pltpu.VMEM((GATHER_WIN, 1), F32),
                            pltpu.SMEM((1,), I32)],
        ),
        out_shape=[
            jax.ShapeDtypeStruct((E, cap, D_MODEL), BF16),
            jax.ShapeDtypeStruct((E, cap, 1), F32),
        ],
        compiler_params=_params("arbitrary", "arbitrary"),
        name="gather",
    )(offs, slot.reshape(E, n_chunks, GATHER_TC), aff_t.reshape(E, n_chunks, GATHER_TC), h2)


def _scatter_window(offs_ref, e, t, cap, level):
    rows, align = level
    return pl.multiple_of(jnp.minimum((offs_ref[e, t] // align) * align, cap - rows), align)


def _scatter_copy(offs_ref, y_hbm, ybuf, sem, e, t, buf, cap, level):
    start = _scatter_window(offs_ref, e, t, cap, level)
    rows = level[0]
    return pltpu.make_async_copy(y_hbm.at[e, pl.ds(start, rows), :],
                                 ybuf.at[buf, pl.ds(e * rows, rows), :], sem.at[buf])


def _scatter_fits(offs_ref, t, level):
    rows, align = level
    fits = True
    for e in range(N_EXPERTS):
        off = offs_ref[e, t]
        fits = fits & ((off % align) + (offs_ref[e, t + 1] - off) <= rows)
    return fits


def _scatter_kernel(offs_ref, slot_ref, x1_ref, y_hbm, o_ref, ybuf, sem, *, cap, levels):
    t = pl.program_id(0)
    buf = t % 2

    def by_level(tt, fn):
        taken = False
        for level in levels[:-1]:
            use = jnp.logical_not(taken) & _scatter_fits(offs_ref, tt, level)
            pl.when(use)(functools.partial(fn, level))
            taken = taken | use
        pl.when(jnp.logical_not(taken))(functools.partial(fn, levels[-1]))

    def start(tt, b, level):
        for e in range(N_EXPERTS):
            _scatter_copy(offs_ref, y_hbm, ybuf, sem, e, tt, b, cap, level).start()

    @pl.when(t == 0)
    def _():
        by_level(t, functools.partial(start, t, buf))

    @pl.when(t + 1 < pl.num_programs(0))
    def _():
        by_level(t + 1, functools.partial(start, t + 1, 1 - buf))

    def finish(level):
        rows = level[0]
        lane = lax.broadcasted_iota(I32, (ROUTE_TB, rows), 1)
        hits = []
        for e in range(N_EXPERTS):
            rel = slot_ref[:, e:e + 1] - _scatter_window(offs_ref, e, t, cap, level)
            hits.append(jnp.where(rel == lane, 1.0, 0.0).astype(BF16))
        onehot = jnp.concatenate(hits, axis=1)
        for e in range(N_EXPERTS):
            _scatter_copy(offs_ref, y_hbm, ybuf, sem, e, t, buf, cap, level).wait()
        o_ref[...] = x1_ref[...] + _dot(onehot, ybuf[buf, pl.ds(0, N_EXPERTS * rows), :])

    by_level(t, finish)


def _scatter(offs, slot, x1, ye):
    E, T = slot.shape
    cap = ye.shape[1]
    levels = tuple((min(r, cap), a) for r, a in SCATTER_LEVELS)
    rows = levels[-1][0]
    return pl.pallas_call(
        functools.partial(_scatter_kernel, cap=cap, levels=levels),
        grid_spec=pltpu.PrefetchScalarGridSpec(
            num_scalar_prefetch=1,
            grid=(T // ROUTE_TB,),
            in_specs=[
                pl.BlockSpec((ROUTE_TB, E), lambda t, offs: (t, 0)),
                pl.BlockSpec((ROUTE_TB, D_MODEL), lambda t, offs: (t, 0)),
                pl.BlockSpec(memory_space=pl.ANY),
            ],
            out_specs=pl.BlockSpec((ROUTE_TB, D_MODEL), lambda t, offs: (t, 0)),
            scratch_shapes=[pltpu.VMEM((2, E * rows, D_MODEL), BF16), pltpu.SemaphoreType.DMA((2,))],
        ),
        out_shape=jax.ShapeDtypeStruct((T, D_MODEL), F32),
        compiler_params=_params("arbitrary"),
        name="scatter",
    )(offs, slot.T, x1, ye)


def _ffn_kernel(xe_ref, gate_ref, wg_ref, wu_ref, wd_ref, o_ref, acc_ref, *, rows):
    f = pl.program_id(1)
    cap = xe_ref.shape[1]

    @pl.when(f == 0)
    def _():
        acc_ref[...] = jnp.zeros_like(acc_ref)

    wg = wg_ref[0].astype(BF16)
    wu = wu_ref[0].astype(BF16)

    def up(i):
        xe = xe_ref[0, i * rows:(i + 1) * rows, :]
        return _dot(xe, wg), _dot(xe, wu)

    nxt = up(0)
    wd = wd_ref[0].astype(BF16)
    for i in range(cap // rows):
        hg, hu = nxt
        if i + 1 < cap // rows:
            nxt = up(i + 1)
        hid = (hg * jax.nn.sigmoid(hg) * hu).astype(BF16)
        acc_ref[i * rows:(i + 1) * rows, :] += _dot(hid, wd)

    @pl.when(f == pl.num_programs(1) - 1)
    def _():
        o_ref[0] = (acc_ref[...] * gate_ref[0]).astype(o_ref.dtype)


def _ffn(xe, gate, w_g, w_u, w_d, *, tf=512, rows=1024):
    E, cap, _ = xe.shape
    rows = min(rows, cap)
    return pl.pallas_call(
        functools.partial(_ffn_kernel, rows=rows),
        grid=(E, EXPERT_FF // tf),
        in_specs=[
            pl.BlockSpec((1, cap, D_MODEL), lambda e, f: (e, 0, 0)),
            pl.BlockSpec((1, cap, 1), lambda e, f: (e, 0, 0)),
            pl.BlockSpec((1, D_MODEL, tf), lambda e, f: (e, 0, f)),
            pl.BlockSpec((1, D_MODEL, tf), lambda e, f: (e, 0, f)),
            pl.BlockSpec((1, tf, D_MODEL), lambda e, f: (e, f, 0)),
        ],
        out_specs=pl.BlockSpec((1, cap, D_MODEL), lambda e, f: (e, 0, 0)),
        out_shape=jax.ShapeDtypeStruct((E, cap, D_MODEL), BF16),
        scratch_shapes=[pltpu.VMEM((cap, D_MODEL), F32)],
        compiler_params=_params("parallel", "arbitrary"),
        name="ffn",
    )(xe, gate, w_g, w_u, w_d)


def _prep_weights(w_in, w_gk_f, w_gk_b, q_norm, k_norm):
    pts = np.cumsum([GLA_QK, GLA_QK, GLA_V, GLA_V, GLA_RANK, GLA_RANK, ATT_Q, ATT_KV, ATT_KV, D_MODEL])
    gq, gk, gv, gg, r_f, r_b, aq, ak, av, z_a, z_b = jnp.split(w_in, pts, axis=-1)
    perm = np.concatenate([np.arange(0, ATT_HD, 2), np.arange(1, ATT_HD, 2)])
    perm_q = (np.arange(ATT_HEADS)[:, None] * ATT_HD + perm[None, :]).reshape(-1)
    perm_k = (np.arange(ATT_KV_HEADS)[:, None] * ATT_HD + perm[None, :]).reshape(-1)
    w_main = jnp.concatenate([gq, gk, gv, gg, z_a, z_b, aq[:, perm_q], ak[:, perm_k], av], axis=-1)
    w_r = jnp.pad(jnp.concatenate([r_f, r_b], axis=-1), ((0, 0), (0, LANE - 2 * GLA_RANK)))
    wf = jnp.pad(w_gk_f, ((0, LANE - GLA_RANK), (0, 0)))
    wb = jnp.pad(w_gk_b, ((GLA_RANK, LANE - 2 * GLA_RANK), (0, 0)))
    return (w_main.astype(BF16), w_r.astype(BF16), wf.astype(BF16), wb.astype(BF16),
            q_norm[perm][None, :], k_norm[perm][None, :])


def _layer(x3, norm_mix, w_main, w_r, wf, bf, wb, bb, gla_norm, qn, kn,
           w_oa, w_ob, w_out, norm_ffn, w_rt, w_e_gate, w_e_up, w_e_down):
    B, S, D = x3.shape
    T = B * S
    x = x3.reshape(T, D)
    proj, r = _inproj(x, norm_mix, w_r, w_main)
    proj3, r3 = proj.reshape(B, S, -1), r.reshape(B, S, LANE)
    o_f = _gla_dir(proj3, r3, wf, bf, True)
    o_a = _gla_dir(proj3, r3, wb, bb, False, o_f, gla_norm)
    cos, sin = _rope_tables(S)
    q, k, v = _qkprep(proj, cos, sin, qn, kn, S)
    o_b = _flash(q.reshape(B, S, ATT_Q), k.reshape(B, S, ATT_KV), v.reshape(B, S, 2 * ATT_KV))
    x1, h2, aff_t = _merge(x, o_a.reshape(T, GLA_V), o_b.reshape(T, ATT_Q), proj,
                           w_oa, w_ob, w_out, norm_ffn, w_rt)
    cap = CAPACITY_FACTOR * T // N_EXPERTS
    slot, offs_gather, offs_scatter = _select(aff_t, cap)
    xe, gate = _gather(offs_gather, slot, aff_t, h2, cap)
    ye = _ffn(xe, gate, w_e_gate, w_e_up, w_e_down)
    out = _scatter(offs_scatter, slot, x1, ye)
    return out.reshape(B, S, D)


def kernel(x_prompt, x_sample, norm_mix, w_in, w_gk_f, b_gk_f, w_gk_b, b_gk_b, gla_norm, q_norm, k_norm,
           w_o_gla, w_o_att, w_out, norm_ffn, w_router, w_e_gate, w_e_up, w_e_down):
    w_main, w_r, wf, wb, qn, kn = _prep_weights(w_in[0], w_gk_f[0], w_gk_b[0], q_norm[0], k_norm[0])
    args = (norm_mix[0][None, :], w_main, w_r, wf, b_gk_f[0][None, :], wb, b_gk_b[0][None, :],
            gla_norm[0][None, :], qn, kn,
            w_o_gla[0].astype(BF16), w_o_att[0].astype(BF16), w_out[0].astype(BF16),
            norm_ffn[0][None, :], w_router[0].T, w_e_gate[0], w_e_up[0], w_e_down[0])
    return (_layer(x_prompt, *args), _layer(x_sample, *args))
```
